```python
import math
import jax, jax.numpy as jnp
from jax import lax
import numpy as np

D_MODEL = 4096
BATCH = 4
SEQ = 2048
DEPTH = 1
DEC_BATCH = 128
DEC_SEQ = 1
PAST_LEN = 16384
PAGE_SIZE = 128

MIX_WIDTH = D_MODEL
GLA_HEADS = 4
GLA_DV = MIX_WIDTH // 2 // GLA_HEADS
GLA_DK = GLA_DV // 2
GLA_RANK = 16
GLA_GATE_NORM = 16.0
RET_HEADS = 4
RET_DV = MIX_WIDTH // 2 // RET_HEADS
RET_DK = RET_DV // 2
ROPE_BASE = 10000.0
CHUNK = 64
N_EXPERTS = 32
TOP_K = 4
D_FF = D_MODEL
SWIGLU_LIMIT = 7.0
SWIGLU_ALPHA = 1.702
EPS = 1e-6

GLA_QK = GLA_HEADS * GLA_DK
GLA_V = GLA_HEADS * GLA_DV
RET_QK = RET_HEADS * RET_DK
RET_V = RET_HEADS * RET_DV
N_IN = 2 * GLA_QK + 2 * GLA_V + GLA_RANK + 2 * RET_QK + 2 * RET_V

kernel_name = "hymba_gla_retnet_moe_step"


def rmsnorm(x, w=None):
    xf = x.astype(jnp.float32)
    y = xf * lax.rsqrt(jnp.mean(xf * xf, axis=-1, keepdims=True) + EPS)
    if w is not None:
        y = y * w.astype(jnp.float32)
    return y


def rotary(x, pos):
    dk = x.shape[-1]
    inv = 1.0 / (ROPE_BASE ** jnp.linspace(0.0, 1.0, dk // 2, dtype=jnp.float32))
    ang = pos[:, None] * inv[None, :]
    cos = jnp.repeat(jnp.cos(ang), 2, axis=-1)[None, :, None, :]
    sin = jnp.repeat(jnp.sin(ang), 2, axis=-1)[None, :, None, :]
    x1 = x[..., 0::2]
    x2 = x[..., 1::2]
    rot = jnp.stack([-x2, x1], axis=-1).reshape(x.shape)
    return x * cos + rot * sin


def _to_chunks(a, n, c):
    B, T, H, d = a.shape
    return jnp.moveaxis(a.reshape(B, n, c, H, d), 1, 0)


def gla_chunked(q, k, v, lg, s0):
    B, T, H, _ = q.shape
    dv = v.shape[-1]
    c = math.gcd(T, CHUNK)
    n = T // c
    causal = jnp.tril(jnp.ones((c, c), dtype=bool))[None, :, :, None, None]

    def step(S, inp):
        qi, ki, vi, gi = inp
        b = jnp.cumsum(gi, axis=1)
        diff = b[:, :, None] - b[:, None, :]
        decay = jnp.exp(jnp.where(causal, diff, -jnp.inf))
        A = jnp.einsum('bthd,btshd,bshd->bhts', qi, decay, ki)
        o = jnp.einsum('bhts,bshv->bthv', A, vi)
        o = o + jnp.einsum('bthd,bhdv->bthv', qi * jnp.exp(b), S)
        b_last = b[:, -1]
        kd = ki * jnp.exp(b_last[:, None] - b)
        S = jnp.exp(b_last)[..., None] * S + jnp.einsum('bshd,bshv->bhdv', kd, vi)
        return S, o

    xs = tuple(_to_chunks(a, n, c) for a in (q, k, v, lg))
    S, o = lax.scan(step, s0, xs)
    o = jnp.moveaxis(o, 0, 1).reshape(B, T, H, dv)
    return o, S


def retention_chunked(q, k, v, log_gamma, s0):
    B, T, H, _ = q.shape
    dv = v.shape[-1]
    c = math.gcd(T, CHUNK)
    n = T // c
    idx = jnp.arange(c, dtype=jnp.float32)
    causal = jnp.tril(jnp.ones((c, c), dtype=bool))
    expo = jnp.where(causal[None], (idx[:, None] - idx[None, :])[None] * log_gamma[:, None, None], -jnp.inf)
    D = jnp.exp(expo)
    q_dec = jnp.exp((idx[:, None] + 1.0) * log_gamma[None, :])[None, :, :, None]
    k_dec = jnp.exp((c - 1.0 - idx[:, None]) * log_gamma[None, :])[None, :, :, None]
    c_dec = jnp.exp(c * log_gamma)[None, :, None, None]

    def step(S, inp):
        qi, ki, vi = inp
        A = jnp.einsum('bthd,bshd->bhts', qi, ki) * D[None]
        o = jnp.einsum('bhts,bshv->bthv', A, vi)
        o = o + jnp.einsum('bthd,bhdv->bthv', qi, S) * q_dec
        S = c_dec * S + jnp.einsum('bshd,bshv->bhdv', ki * k_dec, vi)
        return S, o

    xs = tuple(_to_chunks(a, n, c) for a in (q, k, v))
    S, o = lax.scan(step, s0, xs)
    o = jnp.moveaxis(o, 0, 1).reshape(B, T, H, dv)
    return o, S


def mixer(h, pos0, s_gla, s_ret, w_in, gla_w_gk, gla_b_gk, gla_norm_w, w_out):
    B, T, _ = h.shape
    f32 = jnp.float32
    proj = (h @ w_in).astype(f32)
    cuts = np.cumsum([GLA_QK, GLA_QK, GLA_V, GLA_V, GLA_RANK, RET_QK, RET_QK, RET_V]).tolist()
    gq, gk, gv, gg, gr, rq, rk, rv, rg = jnp.split(proj, cuts, axis=-1)

    q = gq.reshape(B, T, GLA_HEADS, GLA_DK) * (GLA_DK ** -0.5)
    k = gk.reshape(B, T, GLA_HEADS, GLA_DK)
    v = gv.reshape(B, T, GLA_HEADS, GLA_DV)
    z = gr @ gla_w_gk.astype(f32) + gla_b_gk.astype(f32)
    lg = (jax.nn.log_sigmoid(z) / GLA_GATE_NORM).reshape(B, T, GLA_HEADS, GLA_DK)
    o_g, s_gla_new = gla_chunked(q, k, v, lg, s_gla.astype(f32))
    o_g = rmsnorm(o_g, gla_norm_w) * jax.nn.silu(gg.reshape(B, T, GLA_HEADS, GLA_DV))

    pos = pos0 + jnp.arange(T, dtype=f32)
    rqh = rotary(rq.reshape(B, T, RET_HEADS, RET_DK), pos)
    rkh = rotary(rk.reshape(B, T, RET_HEADS, RET_DK), pos) * (RET_DK ** -0.5)
    rvh = rv.reshape(B, T, RET_HEADS, RET_DV)
    log_gamma = jnp.log(1.0 - 2.0 ** (-5.0 - jnp.arange(RET_HEADS, dtype=f32)))
    o_r, s_ret_new = retention_chunked(rqh, rkh, rvh, log_gamma, s_ret.astype(f32))
    o_r = rmsnorm(o_r) * jax.nn.silu(rg.reshape(B, T, RET_HEADS, RET_DV))

    o = jnp.concatenate([o_g.reshape(B, T, GLA_V), o_r.reshape(B, T, RET_V)], axis=-1).astype(h.dtype)
    return o @ w_out, s_gla_new.astype(s_gla.dtype), s_ret_new.astype(s_ret.dtype)


def moe(x, router_w, router_b, w_gate, b_gate, w_up, b_up, w_down, b_down):
    f32 = jnp.float32
    logits = (x @ router_w + router_b).astype(f32)
    top_v, top_i = lax.top_k(logits, TOP_K)
    probs = jax.nn.softmax(top_v, axis=-1)
    comb = jnp.einsum('nk,nke->ne', probs, jax.nn.one_hot(top_i, N_EXPERTS, dtype=f32))

    def body(acc, p):
        wg, bg, wu, bu, wd, bd, cw = p
        g = jnp.minimum((x @ wg + bg).astype(f32), SWIGLU_LIMIT)
        u = jnp.clip((x @ wu + bu).astype(f32), -SWIGLU_LIMIT, SWIGLU_LIMIT)
        hid = ((u + 1.0) * g * jax.nn.sigmoid(SWIGLU_ALPHA * g)).astype(x.dtype)
        out = (hid @ wd + bd).astype(f32)
        return acc + cw[:, None] * out, None

    acc0 = jnp.zeros((x.shape[0], x.shape[1]), f32)
    acc, _ = lax.scan(body, acc0, (w_gate, b_gate, w_up, b_up, w_down, b_down, comb.T))
    return acc.astype(x.dtype)


def setup_inputs(seed: int = 0) -> dict:
    key = jax.random.key(seed)
    ks = jax.random.split(key, 24)
    nrm = lambda k, shape, s: jax.random.normal(k, shape, jnp.float32) * s
    return {
        "x_prompt": nrm(ks[0], (BATCH, SEQ, D_MODEL), 1.0),
        "x_sample": nrm(ks[1], (DEC_BATCH, DEC_SEQ, D_MODEL), 1.0),
        "state_gla": nrm(ks[2], (DEPTH, DEC_BATCH, GLA_HEADS, GLA_DK, GLA_DV), 1.0),
        "state_ret": nrm(ks[3], (DEPTH, DEC_BATCH, RET_HEADS, RET_DK, RET_DV), 1.0),
        "norm_mix": 1.0 + nrm(ks[4], (DEPTH, D_MODEL), 0.02),
        "w_in": nrm(ks[5], (DEPTH, D_MODEL, N_IN), D_MODEL ** -0.5),
        "gla_w_gk": nrm(ks[6], (DEPTH, GLA_RANK, GLA_QK), GLA_RANK ** -0.5),
        "gla_b_gk": nrm(ks[7], (DEPTH, GLA_QK), 0.1),
        "gla_norm_w": 1.0 + nrm(ks[8], (DEPTH, GLA_DV), 0.02),
        "w_out": nrm(ks[9], (DEPTH, MIX_WIDTH, D_MODEL), MIX_WIDTH ** -0.5),
        "norm_ffn": 1.0 + nrm(ks[10], (DEPTH, D_MODEL), 0.02),
        "router_w": nrm(ks[11], (DEPTH, D_MODEL, N_EXPERTS), D_MODEL ** -0.5),
        "router_b": nrm(ks[12], (DEPTH, N_EXPERTS), 0.01),
        "w_gate": nrm(ks[13], (DEPTH, N_EXPERTS, D_MODEL, D_FF), D_MODEL ** -0.5),
        "b_gate": nrm(ks[14], (DEPTH, N_EXPERTS, D_FF), 0.02),
        "w_up": nrm(ks[15], (DEPTH, N_EXPERTS, D_MODEL, D_FF), D_MODEL ** -0.5),
        "b_up": nrm(ks[16], (DEPTH, N_EXPERTS, D_FF), 0.02),
        "w_down": nrm(ks[17], (DEPTH, N_EXPERTS, D_FF, D_MODEL), D_FF ** -0.5),
        "b_down": nrm(ks[18], (DEPTH, N_EXPERTS, D_MODEL), 0.02),
        "norm_final": 1.0 + nrm(ks[19], (D_MODEL,), 0.02),
    }


def reference(x_prompt, x_sample, state_gla, state_ret, norm_mix, w_in, gla_w_gk, gla_b_gk,
              gla_norm_w, w_out, norm_ffn, router_w, router_b, w_gate, b_gate, w_up, b_up,
              w_down, b_down, norm_final):
    hp, hs = x_prompt, x_sample
    Bp, Tp, _ = hp.shape
    Bs, Ts, _ = hs.shape
    gla_p, ret_p, gla_s, ret_s = [], [], [], []
    for l in range(DEPTH):
        zg = jnp.zeros((Bp, GLA_HEADS, GLA_DK, GLA_DV), state_gla.dtype)
        zr = jnp.zeros((Bp, RET_HEADS, RET_DK, RET_DV), state_ret.dtype)
        a_p, sg_p, sr_p = mixer(rmsnorm(hp, norm_mix[l]).astype(hp.dtype), 0.0, zg, zr,
                                w_in[l], gla_w_gk[l], gla_b_gk[l], gla_norm_w[l], w_out[l])
        a_s, sg_s, sr_s = mixer(rmsnorm(hs, norm_mix[l]).astype(hs.dtype), float(PAST_LEN),
                                state_gla[l], state_ret[l],
                                w_in[l], gla_w_gk[l], gla_b_gk[l], gla_norm_w[l], w_out[l])
        hp = hp + a_p
        hs = hs + a_s
        gla_p.append(sg_p); ret_p.append(sr_p); gla_s.append(sg_s); ret_s.append(sr_s)
        flat = jnp.concatenate([rmsnorm(hp, norm_ffn[l]).reshape(Bp * Tp, D_MODEL),
                                rmsnorm(hs, norm_ffn[l]).reshape(Bs * Ts, D_MODEL)], axis=0).astype(hp.dtype)
        f = moe(flat, router_w[l], router_b[l], w_gate[l], b_gate[l], w_up[l], b_up[l],
                w_down[l], b_down[l])
        hp = hp + f[:Bp * Tp].reshape(hp.shape)
        hs = hs + f[Bp * Tp:].reshape(hs.shape)
    y_prompt = rmsnorm(hp, norm_final).astype(x_prompt.dtype)
    y_sample = rmsnorm(hs, norm_final).astype(x_sample.dtype)
    return (y_prompt, y_sample, jnp.stack(gla_p), jnp.stack(ret_p), jnp.stack(gla_s), jnp.stack(ret_s))
```

```python
import functools
import math

import numpy as np
import jax
import jax.numpy as jnp
from jax import lax
from jax.experimental import pallas as pl
from jax.experimental.pallas import tpu as pltpu

F32 = jnp.float32
BF16 = jnp.bfloat16
I32 = jnp.int32

HEADS = 4
GLA_RANK = 16
GLA_GATE_NORM = 16.0
ROPE_BASE = 10000.0
TOP_K = 4
SWIGLU_LIMIT = 7.0
SWIGLU_ALPHA = 1.702
EPS = 1e-6
PAST_LEN = 16384

LANE = 128
ROW_TILE = 128
SLOT_TILES = 10
SLOT_ROWS = SLOT_TILES * ROW_TILE
GLA_CHUNK = 64
GLA_SUB = 16
RET_CHUNK = 256
VMEM_LIMIT = 60 * 1024 * 1024

_NT = (((1,), (1,)), ((), ()))
_TN = (((0,), (0,)), ((), ()))


def _params(sem, vmem=VMEM_LIMIT):
    return pltpu.CompilerParams(dimension_semantics=sem, vmem_limit_bytes=vmem)


def _sigmoid(x):
    return 1.0 / (1.0 + jnp.exp(-x))


def _rms_scale(x):
    return lax.rsqrt(jnp.mean(x * x, axis=-1, keepdims=True) + EPS)


def _in_proj_kernel(x_ref, nw_ref, w_ref, wr_ref, wgk_ref, bgk_ref, proj_ref, lg_ref, h_scr):
    @pl.when(pl.program_id(1) == 0)
    def _():
        x = x_ref[...]
        hb = (x * _rms_scale(x) * nw_ref[...]).astype(BF16)
        h_scr[...] = hb
        gr = jnp.dot(hb, wr_ref[...], preferred_element_type=F32)
        z = jnp.dot(gr.astype(BF16), wgk_ref[...], preferred_element_type=F32) + bgk_ref[...]
        log_sig = jnp.minimum(z, 0.0) - jnp.log1p(jnp.exp(-jnp.abs(z)))
        lg_ref[...] = log_sig * (1.0 / GLA_GATE_NORM)

    proj_ref[...] = jnp.dot(h_scr[...], w_ref[...], preferred_element_type=F32)


def _in_proj(x, nw, w, wr, wgk, bgk, tm, tn):
    m, d = x.shape
    n = w.shape[1]
    nlg = wgk.shape[1]
    return pl.pallas_call(
        _in_proj_kernel,
        grid=(m // tm, n // tn),
        in_specs=[
            pl.BlockSpec((tm, d), lambda i, j: (i, 0)),
            pl.BlockSpec((1, d), lambda i, j: (0, 0)),
            pl.BlockSpec((d, tn), lambda i, j: (0, j)),
            pl.BlockSpec((d, LANE), lambda i, j: (0, 0)),
            pl.BlockSpec((LANE, nlg), lambda i, j: (0, 0)),
            pl.BlockSpec((1, nlg), lambda i, j: (0, 0)),
        ],
        out_specs=[
            pl.BlockSpec((tm, tn), lambda i, j: (i, j)),
            pl.BlockSpec((tm, nlg), lambda i, j: (i, 0)),
        ],
        out_shape=[jax.ShapeDtypeStruct((m, n), F32), jax.ShapeDtypeStruct((m, nlg), F32)],
        scratch_shapes=[pltpu.VMEM((tm, d), BF16)],
        compiler_params=_params(("parallel", "arbitrary")),
        name="in_proj",
    )(x, nw, w, wr, wgk, bgk)


def _rope(x, cos, sin_signed, even):
    n = x.shape[-1]
    nxt = pltpu.roll(x, n - 1, axis=x.ndim - 1)
    prv = pltpu.roll(x, 1, axis=x.ndim - 1)
    return x * cos + jnp.where(even, nxt, prv) * sin_signed


def _ret_prompt_kernel(q_ref, k_ref, v_ref, g_ref, cos_ref, sin_ref, o_ref, s_ref):
    c = q_ref.shape[0]
    dk = q_ref.shape[1] // HEADS
    dv = v_ref.shape[1] // HEADS

    @pl.when(pl.program_id(1) == 0)
    def _():
        s_ref[...] = jnp.zeros_like(s_ref)

    ti = lax.broadcasted_iota(I32, (c, c), 0)
    si = lax.broadcasted_iota(I32, (c, c), 1)
    dlt = (ti - si).astype(F32)
    causal = ti >= si
    tcol = lax.broadcasted_iota(I32, (c, 1), 0).astype(F32)
    even = (lax.broadcasted_iota(I32, (c, dk), 1) % 2) == 0
    cos = cos_ref[...]
    sin = sin_ref[...]
    for h in range(HEADS):
        lgam = math.log(1.0 - 2.0 ** (-5.0 - h))
        q = _rope(q_ref[:, h * dk:(h + 1) * dk], cos, sin, even)
        k = _rope(k_ref[:, h * dk:(h + 1) * dk], cos, sin, even) * (dk ** -0.5)
        qb = q.astype(BF16)
        vb = v_ref[:, h * dv:(h + 1) * dv].astype(BF16)
        decay = jnp.where(causal, jnp.exp(dlt * lgam), 0.0)
        a = lax.dot_general(qb, k.astype(BF16), _NT, preferred_element_type=F32) * decay
        s = s_ref[0, h]
        o = jnp.dot(a.astype(BF16), vb, preferred_element_type=F32)
        o = o + jnp.dot(qb, s.astype(BF16), preferred_element_type=F32) * jnp.exp((tcol + 1.0) * lgam)
        kd = (k * jnp.exp((c - 1.0 - tcol) * lgam)).astype(BF16)
        s_ref[0, h] = math.exp(c * lgam) * s + lax.dot_general(kd, vb, _TN, preferred_element_type=F32)
        g = g_ref[:, h * dv:(h + 1) * dv]
        o_ref[:, h * dv:(h + 1) * dv] = (o * _rms_scale(o) * (g * _sigmoid(g))).astype(BF16)


def _ret_prompt(proj, cos, sin, batch, seq, dk, dv, col_q, col_k, col_v, col_g):
    c = math.gcd(seq, RET_CHUNK)
    nt = seq // c
    qw, vw = HEADS * dk, HEADS * dv
    return pl.pallas_call(
        _ret_prompt_kernel,
        grid=(batch, nt),
        in_specs=[
            pl.BlockSpec((c, qw), lambda b, t: (b * nt + t, col_q // qw)),
            pl.BlockSpec((c, qw), lambda b, t: (b * nt + t, col_k // qw)),
            pl.BlockSpec((c, vw), lambda b, t: (b * nt + t, col_v // vw)),
            pl.BlockSpec((c, vw), lambda b, t: (b * nt + t, col_g // vw)),
            pl.BlockSpec((c, dk), lambda b, t: (t, 0)),
            pl.BlockSpec((c, dk), lambda b, t: (t, 0)),
        ],
        out_specs=[
            pl.BlockSpec((c, vw), lambda b, t: (b * nt + t, 0)),
            pl.BlockSpec((1, HEADS, dk, dv), lambda b, t: (b, 0, 0, 0)),
        ],
        out_shape=[jax.ShapeDtypeStruct((batch * seq, vw), BF16),
                   jax.ShapeDtypeStruct((batch, HEADS, dk, dv), F32)],
        compiler_params=_params(("parallel", "arbitrary")),
        name="ret_prompt",
    )(proj, proj, proj, proj, cos, sin)


def _gla_prompt_kernel(q_ref, k_ref, v_ref, g_ref, lg_ref, nw_ref, o_ref, st_ref, b_scr, a_scr):
    c = q_ref.shape[0]
    dk = q_ref.shape[1] // HEADS
    dv = v_ref.shape[1] // HEADS
    sub = GLA_SUB
    nsub = c // sub

    @pl.when(pl.program_id(1) == 0)
    def _():
        st_ref[...] = jnp.zeros_like(st_ref)

    ri = lax.broadcasted_iota(I32, (c, c), 0)
    ci = lax.broadcasted_iota(I32, (c, c), 1)
    lower = (ri >= ci).astype(BF16)
    cis = lax.broadcasted_iota(I32, (sub, c), 1)

    for h in range(HEADS):
        lg = lg_ref[:, h * dk:(h + 1) * dk]
        l1 = lg.astype(BF16)
        r1 = lg - l1.astype(F32)
        l2 = r1.astype(BF16)
        l3 = (r1 - l2.astype(F32)).astype(BF16)
        b_scr[h] = (jnp.dot(lower, l1, preferred_element_type=F32)
                    + jnp.dot(lower, l2, preferred_element_type=F32)
                    + jnp.dot(lower, l3, preferred_element_type=F32))

    def sub_block(i, carry):
        r0 = pl.multiple_of(i * sub, sub)
        rprev = jnp.maximum(r0 - 1, 0)
        for h in range(HEADS):
            hs = slice(h * dk, (h + 1) * dk)
            qi = q_ref[pl.ds(r0, sub), hs] * (dk ** -0.5)
            ki = k_ref[pl.ds(r0, sub), hs]
            bi = b_scr[h, pl.ds(r0, sub), :]
            ad = jnp.zeros((sub, c), F32)
            for s in range(sub):
                e = jnp.exp(jnp.minimum(bi - bi[s:s + 1, :], 0.0))
                col = jnp.sum(qi * e * ki[s:s + 1, :], axis=1, keepdims=True)
                ad = jnp.where(cis == r0 + s, col, ad)
            bref = b_scr[h, pl.ds(rprev, 1), :]
            qq = (qi * jnp.exp(jnp.minimum(bi - bref, 0.0))).astype(BF16)
            kk = (k_ref[:, hs] * jnp.exp(jnp.minimum(bref - b_scr[h], 0.0))).astype(BF16)
            ao = lax.dot_general(qq, kk, _NT, preferred_element_type=F32)
            a_scr[h, pl.ds(r0, sub), :] = jnp.where(cis < r0, ao, ad)
        return carry

    lax.fori_loop(0, nsub, sub_block, 0)

    for h in range(HEADS):
        hs = slice(h * dk, (h + 1) * dk)
        vs = slice(h * dv, (h + 1) * dv)
        b = b_scr[h]
        b_last = b[c - 1:c, :]
        vb = v_ref[:, vs].astype(BF16)
        qe = (q_ref[:, hs] * (dk ** -0.5) * jnp.exp(b)).astype(BF16)
        kd = (k_ref[:, hs] * jnp.exp(b_last - b)).astype(BF16)
        st = st_ref[0, h]
        a = jnp.where(ri >= ci, a_scr[h], 0.0)
        o = jnp.dot(a.astype(BF16), vb, preferred_element_type=F32)
        o = o + lax.dot_general(qe, st.astype(BF16), _NT, preferred_element_type=F32)
        st_ref[0, h] = st * jnp.exp(b_last) + lax.dot_general(vb, kd, _TN, preferred_element_type=F32)
        g = g_ref[:, vs]
        o_ref[:, vs] = (o * _rms_scale(o) * nw_ref[...] * (g * _sigmoid(g))).astype(BF16)


def _gla_prompt(proj, lg, nw, batch, seq, dk, dv, col_q, col_k, col_v, col_g):
    c = math.gcd(seq, GLA_CHUNK)
    nt = seq // c
    qw, vw = HEADS * dk, HEADS * dv
    return pl.pallas_call(
        _gla_prompt_kernel,
        grid=(batch, nt),
        in_specs=[
            pl.BlockSpec((c, qw), lambda b, t: (b * nt + t, col_q // qw)),
            pl.BlockSpec((c, qw), lambda b, t: (b * nt + t, col_k // qw)),
            pl.BlockSpec((c, vw), lambda b, t: (b * nt + t, col_v // vw)),
            pl.BlockSpec((c, vw), lambda b, t: (b * nt + t, col_g // vw)),
            pl.BlockSpec((c, qw), lambda b, t: (b * nt + t, 0)),
            pl.BlockSpec((1, dv), lambda b, t: (0, 0)),
        ],
        out_specs=[
            pl.BlockSpec((c, vw), lambda b, t: (b * nt + t, 0)),
            pl.BlockSpec((1, HEADS, dv, dk), lambda b, t: (b, 0, 0, 0)),
        ],
        out_shape=[jax.ShapeDtypeStruct((batch * seq, vw), BF16),
                   jax.ShapeDtypeStruct((batch, HEADS, dv, dk), F32)],
        scratch_shapes=[pltpu.VMEM((HEADS, c, dk), F32), pltpu.VMEM((HEADS, c, c), F32)],
        compiler_params=_params(("parallel", "arbitrary")),
        name="gla_prompt",
    )(proj, proj, proj, proj, lg, nw)


def _to_col(row, eye):
    return jnp.sum(jnp.where(eye, row, 0.0), axis=1, keepdims=True)


def _step_kernel(proj_ref, lg_ref, cos_ref, sin_ref, nw_ref, sg_ref, sr_ref,
                 og_ref, or_ref, ng_ref, nr_ref, *, dk, dv, cols):
    b = pl.program_id(0)
    col_gq, col_gk, col_gv, col_gg, col_rq, col_rk, col_rv, col_rg = cols
    eye = lax.broadcasted_iota(I32, (dk, dk), 0) == lax.broadcasted_iota(I32, (dk, dk), 1)
    even = (lax.broadcasted_iota(I32, (1, dk), 1) % 2) == 0
    row = pl.ds(b, 1)

    def finish(o, g, w):
        o = o * _rms_scale(o)
        if w is not None:
            o = o * w
        return o * (g * _sigmoid(g))

    for h in range(HEADS):
        q = proj_ref[row, col_gq + h * dk:col_gq + (h + 1) * dk] * (dk ** -0.5)
        k = proj_ref[row, col_gk + h * dk:col_gk + (h + 1) * dk]
        v = proj_ref[row, col_gv + h * dv:col_gv + (h + 1) * dv]
        g = proj_ref[row, col_gg + h * dv:col_gg + (h + 1) * dv]
        a = jnp.exp(lg_ref[row, h * dk:(h + 1) * dk])
        s_new = _to_col(a, eye) * sg_ref[0, h] + _to_col(k, eye) * v
        ng_ref[0, h] = s_new
        o = jnp.sum(_to_col(q, eye) * s_new, axis=0, keepdims=True)
        og_ref[row, h * dv:(h + 1) * dv] = finish(o, g, nw_ref[...])

        gamma = 1.0 - 2.0 ** (-5.0 - h)
        q = _rope(proj_ref[row, col_rq + h * dk:col_rq + (h + 1) * dk], cos_ref[...], sin_ref[...], even)
        k = _rope(proj_ref[row, col_rk + h * dk:col_rk + (h + 1) * dk], cos_ref[...], sin_ref[...], even)
        k = k * (dk ** -0.5)
        v = proj_ref[row, col_rv + h * dv:col_rv + (h + 1) * dv]
        g = proj_ref[row, col_rg + h * dv:col_rg + (h + 1) * dv]
        s_new = gamma * sr_ref[0, h] + _to_col(k, eye) * v
        nr_ref[0, h] = s_new
        o = jnp.sum(_to_col(q, eye) * s_new, axis=0, keepdims=True)
        or_ref[row, h * dv:(h + 1) * dv] = finish(o, g, None)


def _step(proj, lg, cos, sin, nw, s_gla, s_ret, dk, dv, cols):
    nb, ncol = proj.shape
    vw = HEADS * dv
    state_spec = pl.BlockSpec((1, HEADS, dk, dv), lambda b: (b, 0, 0, 0))
    full = lambda shape: pl.BlockSpec(shape, lambda b: (0,) * len(shape))
    return pl.pallas_call(
        functools.partial(_step_kernel, dk=dk, dv=dv, cols=cols),
        grid=(nb,),
        in_specs=[full((nb, ncol)), full(lg.shape), full((1, dk)), full((1, dk)), full((1, dv)),
                  state_spec, state_spec],
        out_specs=[full((nb, vw)), full((nb, vw)), state_spec, state_spec],
        out_shape=[jax.ShapeDtypeStruct((nb, vw), F32), jax.ShapeDtypeStruct((nb, vw), F32),
                   jax.ShapeDtypeStruct(s_gla.shape, F32), jax.ShapeDtypeStruct(s_ret.shape, F32)],
        compiler_params=_params(("arbitrary",)),
        name="sample_step",
    )(proj, lg, cos, sin, nw, s_gla, s_ret)


def _out_proj_kernel(og_ref, or_ref, x_ref, wg_ref, wr_ref, nw_ref, rwt_ref, rb_ref, *rest):
    h1_ref, lt_ref = rest[-2], rest[-1]
    acc = jnp.dot(og_ref[...].astype(BF16), wg_ref[...], preferred_element_type=F32)
    acc = acc + jnp.dot(or_ref[...].astype(BF16), wr_ref[...], preferred_element_type=F32)
    h1 = x_ref[...] + acc
    h1_ref[...] = h1
    flat = (h1 * _rms_scale(h1) * nw_ref[...]).astype(BF16)
    lt_ref[...] = lax.dot_general(rwt_ref[...], flat, _NT, preferred_element_type=F32) + rb_ref[...]


def _out_proj(og, orr, x, wg, wr, nw, rwt, rb, tm, n_total, row_block0, h1_prev=None):
    m, d = x.shape
    half = og.shape[1]
    ne = rwt.shape[0]
    const = lambda shape: pl.BlockSpec(shape, lambda i: (0,) * len(shape), pipeline_mode=pl.Buffered(1))
    in_specs = [
        pl.BlockSpec((tm, half), lambda i: (i, 0)),
        pl.BlockSpec((tm, half), lambda i: (i, 0)),
        pl.BlockSpec((tm, d), lambda i: (i, 0)),
        const((half, d)), const((half, d)), const((1, d)), const((ne, d)), const((ne, 1)),
    ]
    args = [og, orr, x, wg, wr, nw, rwt, rb]
    aliases = {}
    if h1_prev is not None:
        in_specs.append(pl.BlockSpec(memory_space=pl.ANY))
        args.append(h1_prev)
        aliases = {len(args) - 1: 0}
    return pl.pallas_call(
        _out_proj_kernel,
        grid=(m // tm,),
        in_specs=in_specs,
        out_specs=[
            pl.BlockSpec((tm, d), lambda i: (row_block0 + i, 0)),
            pl.BlockSpec((ne, tm), lambda i: (0, i)),
        ],
        out_shape=[jax.ShapeDtypeStruct((n_total, d), F32), jax.ShapeDtypeStruct((ne, m), F32)],
        input_output_aliases=aliases,
        compiler_params=_params(("parallel",)),
        name="out_proj",
    )(*args)


def _route_kernel(lt_ref, dest_ref, prob_ref, item_ref, tile_ref, m_scr, pos_scr):
    ne, n = lt_ref.shape
    nblk = n // LANE
    logits = lt_ref[...]
    eio = lax.broadcasted_iota(I32, (ne, n), 0).astype(F32)
    vals, idxs = [], []
    for _ in range(TOP_K):
        m = jnp.max(logits, axis=0, keepdims=True)
        ik = jnp.min(jnp.where(logits == m, eio, float(ne)), axis=0, keepdims=True)
        vals.append(m)
        idxs.append(ik)
        logits = jnp.where(eio == ik, -jnp.inf, logits)
    ex = [jnp.exp(v - vals[0]) for v in vals]
    den = ex[0] + ex[1] + ex[2] + ex[3]
    for k in range(TOP_K):
        prob_ref[k:k + 1, :] = ex[k] / den
    sel = jnp.zeros((ne, n), F32)
    for ik in idxs:
        sel = jnp.where(eio == ik, 1.0, sel)
    m_scr[...] = sel

    upper = (lax.broadcasted_iota(I32, (LANE, LANE), 0) <= lax.broadcasted_iota(I32, (LANE, LANE), 1)).astype(BF16)

    def prefix(cb, carry):
        c0 = pl.multiple_of(cb * LANE, LANE)
        mc = m_scr[:, pl.ds(c0, LANE)]
        incl = jnp.dot(mc.astype(BF16), upper, preferred_element_type=F32)
        pos_scr[:, pl.ds(c0, LANE)] = incl - mc + carry
        return carry + jnp.sum(mc, axis=1, keepdims=True)

    cnt = lax.fori_loop(0, nblk, prefix, jnp.zeros((ne, 1), F32))

    def ceil_div(x, step, most):
        r = jnp.zeros_like(x)
        for mlt in range(most):
            r = r + jnp.where(x > float(mlt * step), 1.0, 0.0)
        return r

    cnt_l = jnp.broadcast_to(cnt, (ne, LANE))
    n_slot = ceil_div(cnt_l, SLOT_ROWS, -(-n // SLOT_ROWS))
    n_tile = ceil_div(cnt_l, ROW_TILE, nblk)
    strict = (lax.broadcasted_iota(I32, (ne, ne), 0) > lax.broadcasted_iota(I32, (ne, ne), 1)).astype(BF16)
    slot0 = jnp.dot(strict, n_slot.astype(BF16), preferred_element_type=F32)
    tile0 = jnp.dot(strict, n_tile.astype(BF16), preferred_element_type=F32)

    destf = slot0[:, :1] * float(SLOT_ROWS) + pos_scr[...]
    for k in range(TOP_K):
        dk_ = jnp.sum(jnp.where(eio == idxs[k], destf, 0.0), axis=0, keepdims=True)
        dest_ref[k:k + 1, :] = dk_.astype(I32)

    def lookup(table, onehot):
        return jnp.sum(jnp.where(onehot, table, 0.0), axis=0, keepdims=True)

    e_col = lax.broadcasted_iota(I32, (ne, LANE), 0).astype(F32)
    it = lax.broadcasted_iota(I32, (ne, LANE), 1).astype(F32)
    n_items = jnp.sum(n_slot, axis=0, keepdims=True)
    e_it = jnp.minimum(jnp.sum(jnp.where(slot0 + n_slot <= it, 1.0, 0.0), axis=0, keepdims=True), ne - 1.0)
    oh = e_col == e_it
    rows = lookup(cnt_l, oh) - (it[:1] - lookup(slot0, oh)) * float(SLOT_ROWS)
    rows = jnp.clip(rows, 0.0, float(SLOT_ROWS))
    rows = jnp.where(it[:1] < n_items, rows, 0.0)
    item_ref[...] = jnp.zeros_like(item_ref)
    item_ref[0:1, :] = e_it.astype(I32)
    item_ref[1:2, :] = ceil_div(rows, ROW_TILE, SLOT_TILES).astype(I32)
    item_ref[2:3, :] = n_items.astype(I32)
    item_ref[3:4, :] = jnp.sum(n_tile, axis=0, keepdims=True).astype(I32)

    ntl = tile_ref.shape[1]
    tile_ref[...] = jnp.zeros_like(tile_ref)
    for cb in range(ntl // LANE):
        gi = it + float(cb * LANE)
        e_g = jnp.minimum(jnp.sum(jnp.where(tile0 + n_tile <= gi, 1.0, 0.0), axis=0, keepdims=True), ne - 1.0)
        ohg = e_col == e_g
        tid = lookup(slot0, ohg) * float(SLOT_TILES) + (gi[:1] - lookup(tile0, ohg))
        tile_ref[0:1, cb * LANE:(cb + 1) * LANE] = tid.astype(I32)


def _route(logits_t, n_tiles_max):
    ne, n = logits_t.shape
    ntl = -(-n_tiles_max // LANE) * LANE
    return pl.pallas_call(
        _route_kernel,
        out_shape=[jax.ShapeDtypeStruct((TOP_K, n), I32), jax.ShapeDtypeStruct((TOP_K, n), F32),
                   jax.ShapeDtypeStruct((8, LANE), I32), jax.ShapeDtypeStruct((8, ntl), I32)],
        scratch_shapes=[pltpu.VMEM((ne, n), F32), pltpu.VMEM((ne, n), F32)],
        compiler_params=pltpu.CompilerParams(vmem_limit_bytes=VMEM_LIMIT),
        name="route",
    )(logits_t)


def _gather_kernel(dest_sm, tile_sm, cnt_sm, h1_hbm, nw_ref, xs_ref, inv_sm, buf, sem):
    g = pl.program_id(0)
    n_tiles = cnt_sm[1]
    n_tok = dest_sm.shape[0] // TOP_K

    def issue(gi, slot):
        base = tile_sm[gi] * ROW_TILE

        def body(r, carry):
            src = inv_sm[base + r]
            pltpu.make_async_copy(h1_hbm.at[pl.ds(src, 1)], buf.at[slot, pl.ds(r, 1)], sem.at[slot]).start()
            return carry

        lax.fori_loop(0, ROW_TILE, body, 0, unroll=8)

    @pl.when(g == 0)
    def _():
        def clear(i, carry):
            inv_sm[i] = 0
            return carry

        lax.fori_loop(0, cnt_sm[0] * SLOT_ROWS, clear, 0)
        for k in range(TOP_K):
            def scatter(t, carry):
                inv_sm[dest_sm[k * n_tok + t]] = t
                return carry

            lax.fori_loop(0, n_tok, scatter, 0)
        issue(0, 0)

    @pl.when(g + 1 < n_tiles)
    def _():
        issue(g + 1, (g + 1) % 2)

    @pl.when(g < n_tiles)
    def _():
        slot = g % 2
        pltpu.make_async_copy(h1_hbm.at[pl.ds(0, ROW_TILE)], buf.at[slot], sem.at[slot]).wait()
        x = buf[slot]
        xs_ref[...] = (x * _rms_scale(x) * nw_ref[...]).astype(BF16)


def _gather(dest_flat, tile_ids, counts, h1, nw, n_slots, n_tiles_max):
    d = h1.shape[1]

    def out_map(g, dest_sm, tile_sm, cnt_sm):
        return (tile_sm[jnp.minimum(g, cnt_sm[1] - 1)], 0)

    grid_spec = pltpu.PrefetchScalarGridSpec(
        num_scalar_prefetch=3,
        grid=(n_tiles_max,),
        in_specs=[pl.BlockSpec(memory_space=pl.ANY),
                  pl.BlockSpec((1, d), lambda g, *_: (0, 0))],
        out_specs=pl.BlockSpec((ROW_TILE, d), out_map),
        scratch_shapes=[pltpu.SMEM((n_slots * SLOT_ROWS,), I32),
                        pltpu.VMEM((2, ROW_TILE, d), F32),
                        pltpu.SemaphoreType.DMA((2,))],
    )
    return pl.pallas_call(
        _gather_kernel,
        grid_spec=grid_spec,
        out_shape=jax.ShapeDtypeStruct((n_slots * SLOT_ROWS, d), BF16),
        compiler_params=_params(("arbitrary",)),
        name="moe_gather",
    )(dest_flat, tile_ids, counts, h1, nw)


def _moe_up_kernel(exp_sm, nsub_sm, cnt_sm, x_ref, wg_ref, wu_ref, bg_ref, bu_ref, hid_ref, w_scr):
    it = pl.program_id(0)
    tn = wg_ref.shape[2]

    @pl.when(it < cnt_sm[0])
    def _():
        w_scr[:, :tn] = wg_ref[0].astype(BF16)
        w_scr[:, tn:] = wu_ref[0].astype(BF16)
        ns = nsub_sm[it]

        def body(s, carry):
            r0 = pl.multiple_of(s * ROW_TILE, ROW_TILE)
            gu = jnp.dot(x_ref[0, pl.ds(r0, ROW_TILE), :], w_scr[...], preferred_element_type=F32)
            g = jnp.minimum(gu[:, :tn] + bg_ref[0], SWIGLU_LIMIT)
            u = jnp.clip(gu[:, tn:] + bu_ref[0], -SWIGLU_LIMIT, SWIGLU_LIMIT)
            hid = (u + 1.0) * g * _sigmoid(SWIGLU_ALPHA * g)
            hid_ref[0, pl.ds(r0, ROW_TILE), :] = hid.astype(BF16)
            return carry

        lax.fori_loop(0, ns, body, 0)

        def clear(s, carry):
            r0 = pl.multiple_of(s * ROW_TILE, ROW_TILE)
            hid_ref[0, pl.ds(r0, ROW_TILE), :] = jnp.zeros((ROW_TILE, tn), BF16)
            return carry

        lax.fori_loop(ns, SLOT_TILES, clear, 0)


def _moe_down_kernel(exp_sm, nsub_sm, cnt_sm, h_ref, w_ref, b_ref, y_ref, w_scr):
    it = pl.program_id(0)
    tn = w_ref.shape[2]

    @pl.when(it < cnt_sm[0])
    def _():
        w_scr[...] = w_ref[0].astype(BF16)
        ns = nsub_sm[it]

        def body(s, carry):
            r0 = pl.multiple_of(s * ROW_TILE, ROW_TILE)
            y = jnp.dot(h_ref[0, pl.ds(r0, ROW_TILE), :], w_scr[...], preferred_element_type=F32)
            y_ref[0, pl.ds(r0, ROW_TILE), :] = y + b_ref[0]
            return carry

        lax.fori_loop(0, ns, body, 0)

        def clear(s, carry):
            r0 = pl.multiple_of(s * ROW_TILE, ROW_TILE)
            y_ref[0, pl.ds(r0, ROW_TILE), :] = jnp.zeros((ROW_TILE, tn), F32)
            return carry

        lax.fori_loop(ns, SLOT_TILES, clear, 0)


def _item_maps(nj):
    def eff(it, j, cnt_sm):
        last = cnt_sm[0] - 1
        return jnp.minimum(it, last), jnp.where(it <= last, j, nj - 1)

    def x_map(it, j, exp_sm, nsub_sm, cnt_sm):
        ie, _ = eff(it, j, cnt_sm)
        return (ie, 0, 0)

    def w_map(it, j, exp_sm, nsub_sm, cnt_sm):
        ie, je = eff(it, j, cnt_sm)
        return (exp_sm[ie], 0, je)

    def o_map(it, j, exp_sm, nsub_sm, cnt_sm):
        ie, je = eff(it, j, cnt_sm)
        return (ie, 0, je)

    return x_map, w_map, o_map


def _moe_up(experts, nsub, counts, xs, w_gate, w_up, b_gate, b_up, tn):
    n_slots, rows, d = xs.shape
    dff = w_gate.shape[2]
    nj = dff // tn
    x_map, w_map, o_map = _item_maps(nj)
    grid_spec = pltpu.PrefetchScalarGridSpec(
        num_scalar_prefetch=3,
        grid=(n_slots, nj),
        in_specs=[pl.BlockSpec((1, rows, d), x_map, pipeline_mode=pl.Buffered(1)),
                  pl.BlockSpec((1, d, tn), w_map), pl.BlockSpec((1, d, tn), w_map),
                  pl.BlockSpec((1, 1, tn), w_map), pl.BlockSpec((1, 1, tn), w_map)],
        out_specs=pl.BlockSpec((1, rows, tn), o_map),
        scratch_shapes=[pltpu.VMEM((d, 2 * tn), BF16)],
    )
    return pl.pallas_call(
        _moe_up_kernel,
        grid_spec=grid_spec,
        out_shape=jax.ShapeDtypeStruct((n_slots, rows, dff), BF16),
        compiler_params=_params(("arbitrary", "arbitrary")),
        name="moe_up",
    )(experts, nsub, counts, xs, w_gate, w_up, b_gate, b_up)


def _moe_down(experts, nsub, counts, hid, w_down, b_down, tn):
    n_slots, rows, dff = hid.shape
    d = w_down.shape[2]
    nj = d // tn
    x_map, w_map, o_map = _item_maps(nj)
    grid_spec = pltpu.PrefetchScalarGridSpec(
        num_scalar_prefetch=3,
        grid=(n_slots, nj),
        in_specs=[pl.BlockSpec((1, rows, dff), x_map, pipeline_mode=pl.Buffered(1)),
                  pl.BlockSpec((1, dff, tn), w_map),
                  pl.BlockSpec((1, 1, tn), w_map)],
        out_specs=pl.BlockSpec((1, rows, tn), o_map),
        scratch_shapes=[pltpu.VMEM((dff, tn), BF16)],
    )
    return pl.pallas_call(
        _moe_down_kernel,
        grid_spec=grid_spec,
        out_shape=jax.ShapeDtypeStruct((n_slots, rows, d), F32),
        compiler_params=_params(("arbitrary", "arbitrary")),
        name="moe_down",
    )(experts, nsub, counts, hid, w_down, b_down)


def _combine_kernel(dest_sm, y_hbm, p_ref, h1_ref, nw_ref, yp_ref, ys_ref, buf, sem, *, n_prompt_tiles):
    i = pl.program_id(0)
    n_steps = pl.num_programs(0)
    n_tok = dest_sm.shape[0] // TOP_K

    def issue(ti, slot):
        for k in range(TOP_K):
            def body(r, carry):
                src = dest_sm[k * n_tok + ti * ROW_TILE + r]
                pltpu.make_async_copy(y_hbm.at[pl.ds(src, 1)], buf.at[slot, k, pl.ds(r, 1)], sem.at[slot]).start()
                return carry

            lax.fori_loop(0, ROW_TILE, body, 0, unroll=8)

    @pl.when(i == 0)
    def _():
        issue(0, 0)

    @pl.when(i + 1 < n_steps)
    def _():
        issue(i + 1, (i + 1) % 2)

    slot = i % 2
    for k in range(TOP_K):
        pltpu.make_async_copy(y_hbm.at[pl.ds(0, ROW_TILE)], buf.at[slot, k], sem.at[slot]).wait()
    acc = h1_ref[...]
    for k in range(TOP_K):
        acc = acc + p_ref[:, k:k + 1] * buf[slot, k]
    y = acc * _rms_scale(acc) * nw_ref[...]

    @pl.when(i < n_prompt_tiles)
    def _():
        yp_ref[...] = y

    @pl.when(i >= n_prompt_tiles)
    def _():
        ys_ref[...] = y


def _combine(dest_flat, y_rows, probs, h1, nw, n_prompt, n_sample):
    n, d = h1.shape
    npt = n_prompt // ROW_TILE
    grid_spec = pltpu.PrefetchScalarGridSpec(
        num_scalar_prefetch=1,
        grid=(n // ROW_TILE,),
        in_specs=[pl.BlockSpec(memory_space=pl.ANY),
                  pl.BlockSpec((ROW_TILE, TOP_K), lambda i, *_: (i, 0)),
                  pl.BlockSpec((ROW_TILE, d), lambda i, *_: (i, 0)),
                  pl.BlockSpec((1, d), lambda i, *_: (0, 0))],
        out_specs=[pl.BlockSpec((ROW_TILE, d), lambda i, *_: (jnp.minimum(i, npt - 1), 0)),
                   pl.BlockSpec((ROW_TILE, d), lambda i, *_: (jnp.maximum(i - npt, 0), 0))],
        scratch_shapes=[pltpu.VMEM((2, TOP_K, ROW_TILE, d), F32), pltpu.SemaphoreType.DMA((2,))],
    )
    return pl.pallas_call(
        functools.partial(_combine_kernel, n_prompt_tiles=npt),
        grid_spec=grid_spec,
        out_shape=[jax.ShapeDtypeStruct((n_prompt, d), F32), jax.ShapeDtypeStruct((n_sample, d), F32)],
        compiler_params=_params(("arbitrary",)),
        name="moe_combine",
    )(dest_flat, y_rows, probs, h1, nw)


def _rope_tables(pos0, t, dk):
    inv = 1.0 / (ROPE_BASE ** jnp.linspace(0.0, 1.0, dk // 2, dtype=F32))
    ang = (pos0 + jnp.arange(t, dtype=F32))[:, None] * inv[None, :]
    cos = jnp.repeat(jnp.cos(ang), 2, axis=-1)
    sin = jnp.repeat(jnp.sin(ang), 2, axis=-1)
    sign = jnp.where(jnp.arange(dk) % 2 == 0, -1.0, 1.0).astype(F32)
    return cos, sin * sign


def kernel(x_prompt, x_sample, state_gla, state_ret, norm_mix, w_in, gla_w_gk, gla_b_gk, gla_norm_w, w_out,
           norm_ffn, router_w, router_b, w_gate, b_gate, w_up, b_up, w_down, b_down, norm_final):
    bp, tp, d = x_prompt.shape
    bs, ts, _ = x_sample.shape
    assert ts == 1 and w_in.shape[0] == 1
    dk, dv = state_gla.shape[-2], state_gla.shape[-1]
    qk, vv = HEADS * dk, HEADS * dv
    n_p, n_s = bp * tp, bs * ts
    n = n_p + n_s
    ne = router_w.shape[-1]
    assert n_p % ROW_TILE == 0 and n_s % ROW_TILE == 0

    w = w_in[0]
    c_rank = 2 * qk + 2 * vv
    w_main = jnp.concatenate([w[:, :c_rank], w[:, c_rank + GLA_RANK:]], axis=1).astype(BF16)
    w_rank = jnp.pad(w[:, c_rank:c_rank + GLA_RANK], ((0, 0), (0, LANE - GLA_RANK))).astype(BF16)
    w_gk = jnp.pad(gla_w_gk[0], ((0, LANE - GLA_RANK), (0, 0))).astype(BF16)
    b_gk = gla_b_gk[0][None, :]
    cols = (0, qk, 2 * qk, 2 * qk + vv, c_rank, c_rank + qk, c_rank + 2 * qk, c_rank + 2 * qk + vv)
    nmix = norm_mix[0][None, :]
    wo = w_out[0].astype(BF16)
    wo_g, wo_r = wo[:vv], wo[vv:]
    rwt = router_w[0].T.astype(BF16)
    rb = router_b[0][:, None]
    nffn = norm_ffn[0][None, :]
    gnw = gla_norm_w[0][None, :]

    xp = x_prompt.reshape(n_p, d)
    proj_p, lg_p = _in_proj(xp, nmix, w_main, w_rank, w_gk, b_gk, tm=min(512, n_p), tn=1024)
    cos_p, sin_p = _rope_tables(0.0, tp, dk)
    og_p, st_gla_p = _gla_prompt(proj_p, lg_p, gnw, bp, tp, dk, dv, cols[0], cols[1], cols[2], cols[3])
    or_p, s_ret_p = _ret_prompt(proj_p, cos_p, sin_p, bp, tp, dk, dv, cols[4], cols[5], cols[6], cols[7])
    s_gla_p = jnp.swapaxes(st_gla_p, -1, -2)

    xs = x_sample.reshape(n_s, d)
    proj_s, lg_s = _in_proj(xs, nmix, w_main, w_rank, w_gk, b_gk, tm=n_s, tn=1024)
    cos_s, sin_s = _rope_tables(float(PAST_LEN), 1, dk)
    og_s, or_s, s_gla_s, s_ret_s = _step(proj_s, lg_s, cos_s, sin_s, gnw, state_gla[0], state_ret[0], dk, dv, cols)

    tm_p = min(256, n_p)
    h1, lt_p = _out_proj(og_p, or_p, xp, wo_g, wo_r, nffn, rwt, rb, tm_p, n, 0)
    h1, lt_s = _out_proj(og_s, or_s, xs, wo_g, wo_r, nffn, rwt, rb, n_s, n, n_p // n_s, h1_prev=h1)
    logits_t = jnp.concatenate([lt_p, lt_s], axis=1)

    n_slots = (n * TOP_K) // SLOT_ROWS + ne
    n_tiles_max = (n * TOP_K) // ROW_TILE + ne
    dest, probs_t, item_meta, tile_meta = _route(logits_t, n_tiles_max)
    dest_flat = dest.reshape(-1)
    experts = item_meta[0, :n_slots]
    nsub = item_meta[1, :n_slots]
    counts = item_meta[2:4, 0]
    tile_ids = tile_meta[0, :n_tiles_max]

    xs_rows = _gather(dest_flat, tile_ids, counts, h1, nffn, n_slots, n_tiles_max)
    hid = _moe_up(experts, nsub, counts, xs_rows.reshape(n_slots, SLOT_ROWS, d), w_gate[0], w_up[0],
                  b_gate[0][:, None, :], b_up[0][:, None, :], tn=256)
    y_rows = _moe_down(experts, nsub, counts, hid, w_down[0], b_down[0][:, None, :], tn=512)

    y_p, y_s = _combine(dest_flat, y_rows.reshape(n_slots * SLOT_ROWS, d), probs_t.T, h1,
                        norm_final[None, :], n_p, n_s)
    return (y_p.reshape(bp, tp, d), y_s.reshape(bs, ts, d),
            s_gla_p[None], s_ret_p[None], s_gla_s[None], s_ret_s[None])
```

```python
import functools
import math

import numpy as np
import jax
import jax.numpy as jnp
from jax import lax
from jax.experimental import pallas as pl
from jax.experimental.pallas import tpu as pltpu

F32 = jnp.float32
BF16 = jnp.bfloat16
I32 = jnp.int32

HEADS = 4
GLA_RANK = 16
GLA_GATE_NORM = 16.0
ROPE_BASE = 10000.0
TOP_K = 4
SWIGLU_LIMIT = 7.0
SWIGLU_ALPHA = 1.702
EPS = 1e-6
PAST_LEN = 16384

LANE = 128
ROW_TILE = 128
SLOT_TILES = 10
SLOT_ROWS = SLOT_TILES * ROW_TILE
MM_ROWS = 2 * ROW_TILE
SLOT_CHUNKS = SLOT_ROWS // MM_ROWS
GLA_CHUNK = 64
GLA_SUB = 16
RET_CHUNK = 256
VMEM_LIMIT = 60 * 1024 * 1024

_NT = (((1,), (1,)), ((), ()))
_TN = (((0,), (0,)), ((), ()))


def _params(sem, vmem=VMEM_LIMIT):
    return pltpu.CompilerParams(dimension_semantics=sem, vmem_limit_bytes=vmem)


def _sigmoid(x):
    return 1.0 / (1.0 + jnp.exp(-x))


def _rms_scale(x):
    return lax.rsqrt(jnp.mean(x * x, axis=-1, keepdims=True) + EPS)


def _in_proj_kernel(x_ref, nw_ref, w_ref, wr_ref, wgk_ref, bgk_ref, proj_ref, lg_ref, h_scr):
    @pl.when(pl.program_id(1) == 0)
    def _():
        x = x_ref[...]
        hb = (x * _rms_scale(x) * nw_ref[...]).astype(BF16)
        h_scr[...] = hb
        gr = jnp.dot(hb, wr_ref[...], preferred_element_type=F32)
        z = jnp.dot(gr.astype(BF16), wgk_ref[...], preferred_element_type=F32) + bgk_ref[...]
        log_sig = jnp.minimum(z, 0.0) - jnp.log1p(jnp.exp(-jnp.abs(z)))
        lg_ref[...] = log_sig * (1.0 / GLA_GATE_NORM)

    proj_ref[...] = jnp.dot(h_scr[...], w_ref[...], preferred_element_type=F32)


def _in_proj(x, nw, w, wr, wgk, bgk, tm, tn):
    m, d = x.shape
    n = w.shape[1]
    assert m % tm == 0 and n % tn == 0
    nlg = wgk.shape[1]
    return pl.pallas_call(
        _in_proj_kernel,
        grid=(m // tm, n // tn),
        in_specs=[
            pl.BlockSpec((tm, d), lambda i, j: (i, 0)),
            pl.BlockSpec((1, d), lambda i, j: (0, 0)),
            pl.BlockSpec((d, tn), lambda i, j: (0, j)),
            pl.BlockSpec((d, LANE), lambda i, j: (0, 0)),
            pl.BlockSpec((LANE, nlg), lambda i, j: (0, 0)),
            pl.BlockSpec((1, nlg), lambda i, j: (0, 0)),
        ],
        out_specs=[
            pl.BlockSpec((tm, tn), lambda i, j: (i, j)),
            pl.BlockSpec((tm, nlg), lambda i, j: (i, 0)),
        ],
        out_shape=[jax.ShapeDtypeStruct((m, n), F32), jax.ShapeDtypeStruct((m, nlg), F32)],
        scratch_shapes=[pltpu.VMEM((tm, d), BF16)],
        compiler_params=_params(("parallel", "arbitrary")),
        name="in_proj",
    )(x, nw, w, wr, wgk, bgk)


def _rope(x, cos, sin_signed, even):
    n = x.shape[-1]
    nxt = pltpu.roll(x, n - 1, axis=x.ndim - 1)
    prv = pltpu.roll(x, 1, axis=x.ndim - 1)
    return x * cos + jnp.where(even, nxt, prv) * sin_signed


def _ret_prompt_kernel(q_ref, k_ref, v_ref, g_ref, cos_ref, sin_ref, o_ref, s_ref):
    c = q_ref.shape[0]
    dk = q_ref.shape[1] // HEADS
    dv = v_ref.shape[1] // HEADS

    @pl.when(pl.program_id(1) == 0)
    def _():
        s_ref[...] = jnp.zeros_like(s_ref)

    ti = lax.broadcasted_iota(I32, (c, c), 0)
    si = lax.broadcasted_iota(I32, (c, c), 1)
    dlt = (ti - si).astype(F32)
    causal = ti >= si
    tcol = lax.broadcasted_iota(I32, (c, 1), 0).astype(F32)
    even = (lax.broadcasted_iota(I32, (c, dk), 1) % 2) == 0
    cos = cos_ref[...]
    sin = sin_ref[...]
    for h in range(HEADS):
        lgam = math.log(1.0 - 2.0 ** (-5.0 - h))
        q = _rope(q_ref[:, h * dk:(h + 1) * dk], cos, sin, even)
        k = _rope(k_ref[:, h * dk:(h + 1) * dk], cos, sin, even) * (dk ** -0.5)
        qb = q.astype(BF16)
        vb = v_ref[:, h * dv:(h + 1) * dv].astype(BF16)
        decay = jnp.where(causal, jnp.exp(dlt * lgam), 0.0)
        a = lax.dot_general(qb, k.astype(BF16), _NT, preferred_element_type=F32) * decay
        s = s_ref[0, h]
        o = jnp.dot(a.astype(BF16), vb, preferred_element_type=F32)
        o = o + jnp.dot(qb, s.astype(BF16), preferred_element_type=F32) * jnp.exp((tcol + 1.0) * lgam)
        kd = (k * jnp.exp((c - 1.0 - tcol) * lgam)).astype(BF16)
        s_ref[0, h] = math.exp(c * lgam) * s + lax.dot_general(kd, vb, _TN, preferred_element_type=F32)
        g = g_ref[:, h * dv:(h + 1) * dv]
        o_ref[:, h * dv:(h + 1) * dv] = (o * _rms_scale(o) * (g * _sigmoid(g))).astype(BF16)


def _ret_prompt(proj, cos, sin, batch, seq, dk, dv, col_q, col_k, col_v, col_g):
    c = math.gcd(seq, RET_CHUNK)
    nt = seq // c
    qw, vw = HEADS * dk, HEADS * dv
    return pl.pallas_call(
        _ret_prompt_kernel,
        grid=(batch, nt),
        in_specs=[
            pl.BlockSpec((c, qw), lambda b, t: (b * nt + t, col_q // qw)),
            pl.BlockSpec((c, qw), lambda b, t: (b * nt + t, col_k // qw)),
            pl.BlockSpec((c, vw), lambda b, t: (b * nt + t, col_v // vw)),
            pl.BlockSpec((c, vw), lambda b, t: (b * nt + t, col_g // vw)),
            pl.BlockSpec((c, dk), lambda b, t: (t, 0)),
            pl.BlockSpec((c, dk), lambda b, t: (t, 0)),
        ],
        out_specs=[
            pl.BlockSpec((c, vw), lambda b, t: (b * nt + t, 0)),
            pl.BlockSpec((1, HEADS, dk, dv), lambda b, t: (b, 0, 0, 0)),
        ],
        out_shape=[jax.ShapeDtypeStruct((batch * seq, vw), BF16),
                   jax.ShapeDtypeStruct((batch, HEADS, dk, dv), F32)],
        compiler_params=_params(("parallel", "arbitrary")),
        name="ret_prompt",
    )(proj, proj, proj, proj, cos, sin)


def _gla_prompt_kernel(q_ref, k_ref, v_ref, g_ref, lg_ref, nw_ref, o_ref, st_ref, b_scr, a_scr):
    c = q_ref.shape[0]
    dk = q_ref.shape[1] // HEADS
    dv = v_ref.shape[1] // HEADS
    sub = GLA_SUB
    nsub = c // sub

    @pl.when(pl.program_id(1) == 0)
    def _():
        st_ref[...] = jnp.zeros_like(st_ref)

    ri = lax.broadcasted_iota(I32, (c, c), 0)
    ci = lax.broadcasted_iota(I32, (c, c), 1)
    lower = (ri >= ci).astype(BF16)
    cis = lax.broadcasted_iota(I32, (sub, c), 1)

    for h in range(HEADS):
        lg = lg_ref[:, h * dk:(h + 1) * dk]
        l1 = lg.astype(BF16)
        r1 = lg - l1.astype(F32)
        l2 = r1.astype(BF16)
        l3 = (r1 - l2.astype(F32)).astype(BF16)
        b_scr[h] = (jnp.dot(lower, l1, preferred_element_type=F32)
                    + jnp.dot(lower, l2, preferred_element_type=F32)
                    + jnp.dot(lower, l3, preferred_element_type=F32))

    def sub_block(i, carry):
        r0 = pl.multiple_of(i * sub, sub)
        rprev = jnp.maximum(r0 - 1, 0)
        for h in range(HEADS):
            hs = slice(h * dk, (h + 1) * dk)
            qi = q_ref[pl.ds(r0, sub), hs] * (dk ** -0.5)
            ki = k_ref[pl.ds(r0, sub), hs]
            bi = b_scr[h, pl.ds(r0, sub), :]
            ad = jnp.zeros((sub, c), F32)
            for s in range(sub):
                e = jnp.exp(jnp.minimum(bi - bi[s:s + 1, :], 0.0))
                col = jnp.sum(qi * e * ki[s:s + 1, :], axis=1, keepdims=True)
                ad = jnp.where(cis == r0 + s, col, ad)
            bref = b_scr[h, pl.ds(rprev, 1), :]
            qq = (qi * jnp.exp(jnp.minimum(bi - bref, 0.0))).astype(BF16)
            kk = (k_ref[:, hs] * jnp.exp(jnp.minimum(bref - b_scr[h], 0.0))).astype(BF16)
            ao = lax.dot_general(qq, kk, _NT, preferred_element_type=F32)
            a_scr[h, pl.ds(r0, sub), :] = jnp.where(cis < r0, ao, ad)
        return carry

    lax.fori_loop(0, nsub, sub_block, 0)

    for h in range(HEADS):
        hs = slice(h * dk, (h + 1) * dk)
        vs = slice(h * dv, (h + 1) * dv)
        b = b_scr[h]
        b_last = b[c - 1:c, :]
        vb = v_ref[:, vs].astype(BF16)
        qe = (q_ref[:, hs] * (dk ** -0.5) * jnp.exp(b)).astype(BF16)
        kd = (k_ref[:, hs] * jnp.exp(b_last - b)).astype(BF16)
        st = st_ref[0, h]
        a = jnp.where(ri >= ci, a_scr[h], 0.0)
        o = jnp.dot(a.astype(BF16), vb, preferred_element_type=F32)
        o = o + lax.dot_general(qe, st.astype(BF16), _NT, preferred_element_type=F32)
        st_ref[0, h] = st * jnp.exp(b_last) + lax.dot_general(vb, kd, _TN, preferred_element_type=F32)
        g = g_ref[:, vs]
        o_ref[:, vs] = (o * _rms_scale(o) * nw_ref[...] * (g * _sigmoid(g))).astype(BF16)


def _gla_prompt(proj, lg, nw, batch, seq, dk, dv, col_q, col_k, col_v, col_g):
    c = math.gcd(seq, GLA_CHUNK)
    nt = seq // c
    qw, vw = HEADS * dk, HEADS * dv
    return pl.pallas_call(
        _gla_prompt_kernel,
        grid=(batch, nt),
        in_specs=[
            pl.BlockSpec((c, qw), lambda b, t: (b * nt + t, col_q // qw)),
            pl.BlockSpec((c, qw), lambda b, t: (b * nt + t, col_k // qw)),
            pl.BlockSpec((c, vw), lambda b, t: (b * nt + t, col_v // vw)),
            pl.BlockSpec((c, vw), lambda b, t: (b * nt + t, col_g // vw)),
            pl.BlockSpec((c, qw), lambda b, t: (b * nt + t, 0)),
            pl.BlockSpec((1, dv), lambda b, t: (0, 0)),
        ],
        out_specs=[
            pl.BlockSpec((c, vw), lambda b, t: (b * nt + t, 0)),
            pl.BlockSpec((1, HEADS, dv, dk), lambda b, t: (b, 0, 0, 0)),
        ],
        out_shape=[jax.ShapeDtypeStruct((batch * seq, vw), BF16),
                   jax.ShapeDtypeStruct((batch, HEADS, dv, dk), F32)],
        scratch_shapes=[pltpu.VMEM((HEADS, c, dk), F32), pltpu.VMEM((HEADS, c, c), F32)],
        compiler_params=_params(("parallel", "arbitrary")),
        name="gla_prompt",
    )(proj, proj, proj, proj, lg, nw)


def _to_col(row, eye):
    return jnp.sum(jnp.where(eye, row, 0.0), axis=1, keepdims=True)


def _step_kernel(proj_ref, lg_ref, cos_ref, sin_ref, nw_ref, sg_ref, sr_ref,
                 og_ref, or_ref, ng_ref, nr_ref, *, dk, dv, cols):
    b = pl.program_id(0)
    col_gq, col_gk, col_gv, col_gg, col_rq, col_rk, col_rv, col_rg = cols
    eye = lax.broadcasted_iota(I32, (dk, dk), 0) == lax.broadcasted_iota(I32, (dk, dk), 1)
    even = (lax.broadcasted_iota(I32, (1, dk), 1) % 2) == 0
    row = pl.ds(b, 1)

    def finish(o, g, w):
        o = o * _rms_scale(o)
        if w is not None:
            o = o * w
        return o * (g * _sigmoid(g))

    for h in range(HEADS):
        q = proj_ref[row, col_gq + h * dk:col_gq + (h + 1) * dk] * (dk ** -0.5)
        k = proj_ref[row, col_gk + h * dk:col_gk + (h + 1) * dk]
        v = proj_ref[row, col_gv + h * dv:col_gv + (h + 1) * dv]
        g = proj_ref[row, col_gg + h * dv:col_gg + (h + 1) * dv]
        a = jnp.exp(lg_ref[row, h * dk:(h + 1) * dk])
        s_new = _to_col(a, eye) * sg_ref[0, h] + _to_col(k, eye) * v
        ng_ref[0, h] = s_new
        o = jnp.sum(_to_col(q, eye) * s_new, axis=0, keepdims=True)
        og_ref[row, h * dv:(h + 1) * dv] = finish(o, g, nw_ref[...])

        gamma = 1.0 - 2.0 ** (-5.0 - h)
        q = _rope(proj_ref[row, col_rq + h * dk:col_rq + (h + 1) * dk], cos_ref[...], sin_ref[...], even)
        k = _rope(proj_ref[row, col_rk + h * dk:col_rk + (h + 1) * dk], cos_ref[...], sin_ref[...], even)
        k = k * (dk ** -0.5)
        v = proj_ref[row, col_rv + h * dv:col_rv + (h + 1) * dv]
        g = proj_ref[row, col_rg + h * dv:col_rg + (h + 1) * dv]
        s_new = gamma * sr_ref[0, h] + _to_col(k, eye) * v
        nr_ref[0, h] = s_new
        o = jnp.sum(_to_col(q, eye) * s_new, axis=0, keepdims=True)
        or_ref[row, h * dv:(h + 1) * dv] = finish(o, g, None)


def _step(proj, lg, cos, sin, nw, s_gla, s_ret, dk, dv, cols):
    nb, ncol = proj.shape
    vw = HEADS * dv
    state_spec = pl.BlockSpec((1, HEADS, dk, dv), lambda b: (b, 0, 0, 0))
    full = lambda shape: pl.BlockSpec(shape, lambda b: (0,) * len(shape))
    return pl.pallas_call(
        functools.partial(_step_kernel, dk=dk, dv=dv, cols=cols),
        grid=(nb,),
        in_specs=[full((nb, ncol)), full(lg.shape), full((1, dk)), full((1, dk)), full((1, dv)),
                  state_spec, state_spec],
        out_specs=[full((nb, vw)), full((nb, vw)), state_spec, state_spec],
        out_shape=[jax.ShapeDtypeStruct((nb, vw), F32), jax.ShapeDtypeStruct((nb, vw), F32),
                   jax.ShapeDtypeStruct(s_gla.shape, F32), jax.ShapeDtypeStruct(s_ret.shape, F32)],
        compiler_params=_params(("arbitrary",)),
        name="sample_step",
    )(proj, lg, cos, sin, nw, s_gla, s_ret)


def _out_proj_kernel(og_ref, or_ref, x_ref, wg_ref, wr_ref, nw_ref, rwt_ref, rb_ref, *rest):
    h1_ref, lt_ref = rest[-2], rest[-1]
    acc = jnp.dot(og_ref[...].astype(BF16), wg_ref[...], preferred_element_type=F32)
    acc = acc + jnp.dot(or_ref[...].astype(BF16), wr_ref[...], preferred_element_type=F32)
    h1 = x_ref[...] + acc
    h1_ref[...] = h1
    flat = (h1 * _rms_scale(h1) * nw_ref[...]).astype(BF16)
    lt_ref[...] = lax.dot_general(rwt_ref[...], flat, _NT, preferred_element_type=F32) + rb_ref[...]


def _out_proj(og, orr, x, wg, wr, nw, rwt, rb, tm, n_total, row_block0, h1_prev=None):
    m, d = x.shape
    half = og.shape[1]
    ne = rwt.shape[0]
    const = lambda shape: pl.BlockSpec(shape, lambda i: (0,) * len(shape), pipeline_mode=pl.Buffered(1))
    in_specs = [
        pl.BlockSpec((tm, half), lambda i: (i, 0)),
        pl.BlockSpec((tm, half), lambda i: (i, 0)),
        pl.BlockSpec((tm, d), lambda i: (i, 0)),
        const((half, d)), const((half, d)), const((1, d)), const((ne, d)), const((ne, 1)),
    ]
    args = [og, orr, x, wg, wr, nw, rwt, rb]
    aliases = {}
    if h1_prev is not None:
        in_specs.append(pl.BlockSpec(memory_space=pl.ANY))
        args.append(h1_prev)
        aliases = {len(args) - 1: 0}
    return pl.pallas_call(
        _out_proj_kernel,
        grid=(m // tm,),
        in_specs=in_specs,
        out_specs=[
            pl.BlockSpec((tm, d), lambda i: (row_block0 + i, 0)),
            pl.BlockSpec((ne, tm), lambda i: (0, i)),
        ],
        out_shape=[jax.ShapeDtypeStruct((n_total, d), F32), jax.ShapeDtypeStruct((ne, m), F32)],
        input_output_aliases=aliases,
        compiler_params=_params(("parallel",)),
        name="out_proj",
    )(*args)


def _route_kernel(lt_ref, dest_ref, prob_ref, item_ref, tile_ref, m_scr, pos_scr):
    ne, n = lt_ref.shape
    nblk = n // LANE
    logits = lt_ref[...]
    eio = lax.broadcasted_iota(I32, (ne, n), 0).astype(F32)
    vals, idxs = [], []
    for _ in range(TOP_K):
        m = jnp.max(logits, axis=0, keepdims=True)
        ik = jnp.min(jnp.where(logits == m, eio, float(ne)), axis=0, keepdims=True)
        vals.append(m)
        idxs.append(ik)
        logits = jnp.where(eio == ik, -jnp.inf, logits)
    ex = [jnp.exp(v - vals[0]) for v in vals]
    den = ex[0] + ex[1] + ex[2] + ex[3]
    for k in range(TOP_K):
        prob_ref[k:k + 1, :] = ex[k] / den
    sel = jnp.zeros((ne, n), F32)
    for ik in idxs:
        sel = jnp.where(eio == ik, 1.0, sel)
    m_scr[...] = sel

    upper = (lax.broadcasted_iota(I32, (LANE, LANE), 0) <= lax.broadcasted_iota(I32, (LANE, LANE), 1)).astype(BF16)

    def prefix(cb, carry):
        c0 = pl.multiple_of(cb * LANE, LANE)
        mc = m_scr[:, pl.ds(c0, LANE)]
        incl = jnp.dot(mc.astype(BF16), upper, preferred_element_type=F32)
        pos_scr[:, pl.ds(c0, LANE)] = incl - mc + carry
        return carry + jnp.sum(mc, axis=1, keepdims=True)

    cnt = lax.fori_loop(0, nblk, prefix, jnp.zeros((ne, 1), F32))

    def ceil_div(x, step, most):
        r = jnp.zeros_like(x)
        for mlt in range(most):
            r = r + jnp.where(x > float(mlt * step), 1.0, 0.0)
        return r

    cnt_l = jnp.broadcast_to(cnt, (ne, LANE))
    n_slot = ceil_div(cnt_l, SLOT_ROWS, -(-n // SLOT_ROWS))
    n_tile = ceil_div(cnt_l, MM_ROWS, -(-n // MM_ROWS))
    strict = (lax.broadcasted_iota(I32, (ne, ne), 0) > lax.broadcasted_iota(I32, (ne, ne), 1)).astype(BF16)
    slot0 = jnp.dot(strict, n_slot.astype(BF16), preferred_element_type=F32)
    tile0 = jnp.dot(strict, n_tile.astype(BF16), preferred_element_type=F32)

    destf = slot0[:, :1] * float(SLOT_ROWS) + pos_scr[...]
    for k in range(TOP_K):
        dk_ = jnp.sum(jnp.where(eio == idxs[k], destf, 0.0), axis=0, keepdims=True)
        dest_ref[k:k + 1, :] = dk_.astype(I32)

    def lookup(table, onehot):
        return jnp.sum(jnp.where(onehot, table, 0.0), axis=0, keepdims=True)

    e_col = lax.broadcasted_iota(I32, (ne, LANE), 0).astype(F32)
    it = lax.broadcasted_iota(I32, (ne, LANE), 1).astype(F32)
    n_items = jnp.sum(n_slot, axis=0, keepdims=True)
    e_it = jnp.minimum(jnp.sum(jnp.where(slot0 + n_slot <= it, 1.0, 0.0), axis=0, keepdims=True), ne - 1.0)
    oh = e_col == e_it
    rows = lookup(cnt_l, oh) - (it[:1] - lookup(slot0, oh)) * float(SLOT_ROWS)
    rows = jnp.clip(rows, 0.0, float(SLOT_ROWS))
    rows = jnp.where(it[:1] < n_items, rows, 0.0)
    item_ref[...] = jnp.zeros_like(item_ref)
    item_ref[0:1, :] = e_it.astype(I32)
    item_ref[1:2, :] = ceil_div(rows, ROW_TILE, SLOT_TILES).astype(I32)
    item_ref[2:3, :] = n_items.astype(I32)
    item_ref[3:4, :] = jnp.sum(n_tile, axis=0, keepdims=True).astype(I32)
    diag = e_col == it
    item_ref[4:5, :] = lookup(slot0 * float(SLOT_ROWS) + cnt_l, diag).astype(I32)
    item_ref[5:6, :] = lookup(slot0 * float(SLOT_ROWS) + n_tile * float(MM_ROWS), diag).astype(I32)

    ntl = tile_ref.shape[1]
    tile_ref[...] = jnp.zeros_like(tile_ref)
    for cb in range(ntl // LANE):
        gi = it + float(cb * LANE)
        e_g = jnp.minimum(jnp.sum(jnp.where(tile0 + n_tile <= gi, 1.0, 0.0), axis=0, keepdims=True), ne - 1.0)
        ohg = e_col == e_g
        tid = lookup(slot0, ohg) * float(SLOT_CHUNKS) + (gi[:1] - lookup(tile0, ohg))
        tile_ref[0:1, cb * LANE:(cb + 1) * LANE] = tid.astype(I32)


def _route(logits_t, n_tiles_max):
    ne, n = logits_t.shape
    ntl = -(-n_tiles_max // LANE) * LANE
    return pl.pallas_call(
        _route_kernel,
        out_shape=[jax.ShapeDtypeStruct((TOP_K, n), I32), jax.ShapeDtypeStruct((TOP_K, n), F32),
                   jax.ShapeDtypeStruct((8, LANE), I32), jax.ShapeDtypeStruct((8, ntl), I32)],
        scratch_shapes=[pltpu.VMEM((ne, n), F32), pltpu.VMEM((ne, n), F32)],
        compiler_params=pltpu.CompilerParams(vmem_limit_bytes=VMEM_LIMIT),
        name="route",
    )(logits_t)


def _gather_kernel(dest_sm, tile_sm, cnt_sm, pad_lo_sm, pad_hi_sm, h1_hbm, nw_ref, xs_ref, inv_sm, buf, sem):
    g = pl.program_id(0)
    n_tiles = cnt_sm[1]
    n_tok = dest_sm.shape[0] // TOP_K
    group = 16

    def issue(gi, slot):
        base = tile_sm[gi] * MM_ROWS

        def body(r, carry):
            src = inv_sm[base + r]
            pltpu.make_async_copy(h1_hbm.at[pl.ds(src, 1)], buf.at[slot, pl.ds(r, 1)], sem.at[slot]).start()
            return carry

        lax.fori_loop(0, MM_ROWS, body, 0, unroll=8)

    @pl.when(g == 0)
    def _():
        def pad_expert(e, carry):
            def clear(i, c2):
                inv_sm[i] = 0
                return c2

            lax.fori_loop(pad_lo_sm[e], pad_hi_sm[e], clear, 0)
            return carry

        lax.fori_loop(0, pad_lo_sm.shape[0], pad_expert, 0)
        for k in range(TOP_K):
            def scatter(t, carry):
                inv_sm[dest_sm[k * n_tok + t]] = t
                return carry

            lax.fori_loop(0, n_tok, scatter, 0, unroll=8)
        issue(0, 0)

    @pl.when(g + 1 < n_tiles)
    def _():
        issue(g + 1, (g + 1) % 2)

    @pl.when(g < n_tiles)
    def _():
        slot = g % 2
        pltpu.make_async_copy(h1_hbm.at[pl.ds(0, MM_ROWS)], buf.at[slot], sem.at[slot]).wait()

        def norm_rows(r, carry):
            r0 = pl.multiple_of(r * group, group)
            x = buf[slot, pl.ds(r0, group), :]
            xs_ref[pl.ds(r0, group), :] = (x * _rms_scale(x) * nw_ref[...]).astype(BF16)
            return carry

        lax.fori_loop(0, MM_ROWS // group, norm_rows, 0, unroll=2)


def _gather(dest_flat, tile_ids, counts, pad_lo, pad_hi, h1, nw, n_slots, n_tiles_max):
    d = h1.shape[1]

    def out_map(g, dest_sm, tile_sm, cnt_sm, *_):
        return (tile_sm[jnp.minimum(g, cnt_sm[1] - 1)], 0)

    grid_spec = pltpu.PrefetchScalarGridSpec(
        num_scalar_prefetch=5,
        grid=(n_tiles_max,),
        in_specs=[pl.BlockSpec(memory_space=pl.ANY),
                  pl.BlockSpec((1, d), lambda g, *_: (0, 0))],
        out_specs=pl.BlockSpec((MM_ROWS, d), out_map),
        scratch_shapes=[pltpu.SMEM((n_slots * SLOT_ROWS,), I32),
                        pltpu.VMEM((2, MM_ROWS, d), F32),
                        pltpu.SemaphoreType.DMA((2,))],
    )
    return pl.pallas_call(
        _gather_kernel,
        grid_spec=grid_spec,
        out_shape=jax.ShapeDtypeStruct((n_slots * SLOT_ROWS, d), BF16),
        compiler_params=_params(("arbitrary",)),
        name="moe_gather",
    )(dest_flat, tile_ids, counts, pad_lo, pad_hi, h1, nw)


def _for_row_chunks(ns, matmul, finish, clear, acc_scr):
    acc_a, acc_b = acc_scr
    n_full = jnp.maximum(lax.shift_right_logical(ns, 1), 1)
    n_pair = lax.shift_right_logical(n_full - 1, 1)
    row0 = lambda c: pl.multiple_of(c * MM_ROWS, MM_ROWS)
    acc_a[...] = matmul(0, MM_ROWS)

    def pair(p, carry):
        c = 2 * p
        acc_b[...] = matmul(row0(c + 1), MM_ROWS)
        finish(row0(c), MM_ROWS, acc_a[...])
        acc_a[...] = matmul(row0(c + 2), MM_ROWS)
        finish(row0(c + 1), MM_ROWS, acc_b[...])
        return carry

    lax.fori_loop(0, n_pair, pair, 0)
    done = 2 * n_pair
    extra = (n_full - 1) > done
    odd = jnp.logical_and(ns > 1, (ns & 1) == 1)
    t0 = pl.multiple_of((ns - 1) * ROW_TILE, ROW_TILE)

    @pl.when(extra)
    def _():
        acc_b[...] = matmul(row0(done + 1), MM_ROWS)
        finish(row0(done), MM_ROWS, acc_a[...])

    def drain(acc, c):
        @pl.when(odd)
        def _():
            tail = matmul(t0, ROW_TILE)
            finish(row0(c), MM_ROWS, acc[...])
            finish(t0, ROW_TILE, tail)

        @pl.when(jnp.logical_not(odd))
        def _():
            finish(row0(c), MM_ROWS, acc[...])

    @pl.when(extra)
    def _():
        drain(acc_b, done + 1)

    @pl.when(jnp.logical_not(extra))
    def _():
        drain(acc_a, done)

    def clear_tile(s, carry):
        clear(pl.multiple_of(s * ROW_TILE, ROW_TILE))
        return carry

    lax.fori_loop(jnp.maximum(ns, MM_ROWS // ROW_TILE), SLOT_TILES, clear_tile, 0)


def _moe_up_kernel(exp_sm, nsub_sm, cnt_sm, x_ref, wg_ref, wu_ref, bg_ref, bu_ref, hid_ref, w_scr, *acc_scr):
    it = pl.program_id(0)
    tn = wg_ref.shape[2]

    @pl.when(it < cnt_sm[0])
    def _():
        w_scr[:, :tn] = wg_ref[0].astype(BF16)
        w_scr[:, tn:] = wu_ref[0].astype(BF16)

        def matmul(r0, rows):
            return jnp.dot(x_ref[0, pl.ds(r0, rows), :], w_scr[...], preferred_element_type=F32)

        def finish(r0, rows, gu):
            g = jnp.minimum(gu[:, :tn] + bg_ref[0], SWIGLU_LIMIT)
            u = jnp.clip(gu[:, tn:] + bu_ref[0], -SWIGLU_LIMIT, SWIGLU_LIMIT)
            hid = (u + 1.0) * g * _sigmoid(SWIGLU_ALPHA * g)
            hid_ref[0, pl.ds(r0, rows), :] = hid.astype(BF16)

        def clear(r0):
            hid_ref[0, pl.ds(r0, ROW_TILE), :] = jnp.zeros((ROW_TILE, tn), BF16)

        _for_row_chunks(nsub_sm[it], matmul, finish, clear, acc_scr)


def _moe_down_kernel(exp_sm, nsub_sm, cnt_sm, h_ref, w_ref, b_ref, y_ref, w_scr, *acc_scr):
    it = pl.program_id(0)
    tn = w_ref.shape[2]

    @pl.when(it < cnt_sm[0])
    def _():
        w_scr[...] = w_ref[0].astype(BF16)

        def matmul(r0, rows):
            return jnp.dot(h_ref[0, pl.ds(r0, rows), :], w_scr[...], preferred_element_type=F32)

        def finish(r0, rows, y):
            y_ref[0, pl.ds(r0, rows), :] = y + b_ref[0]

        def clear(r0):
            y_ref[0, pl.ds(r0, ROW_TILE), :] = jnp.zeros((ROW_TILE, tn), F32)

        _for_row_chunks(nsub_sm[it], matmul, finish, clear, acc_scr)


def _item_maps(nj):
    def eff(it, j, cnt_sm):
        last = cnt_sm[0] - 1
        return jnp.minimum(it, last), jnp.where(it <= last, j, nj - 1)

    def x_map(it, j, exp_sm, nsub_sm, cnt_sm):
        ie, _ = eff(it, j, cnt_sm)
        return (ie, 0, 0)

    def w_map(it, j, exp_sm, nsub_sm, cnt_sm):
        ie, je = eff(it, j, cnt_sm)
        return (exp_sm[ie], 0, je)

    def o_map(it, j, exp_sm, nsub_sm, cnt_sm):
        ie, je = eff(it, j, cnt_sm)
        return (ie, 0, je)

    return x_map, w_map, o_map


def _moe_up(experts, nsub, counts, xs, w_gate, w_up, b_gate, b_up, tn):
    n_slots, rows, d = xs.shape
    dff = w_gate.shape[2]
    nj = dff // tn
    x_map, w_map, o_map = _item_maps(nj)
    grid_spec = pltpu.PrefetchScalarGridSpec(
        num_scalar_prefetch=3,
        grid=(n_slots, nj),
        in_specs=[pl.BlockSpec((1, rows, d), x_map),
                  pl.BlockSpec((1, d, tn), w_map), pl.BlockSpec((1, d, tn), w_map),
                  pl.BlockSpec((1, 1, tn), w_map), pl.BlockSpec((1, 1, tn), w_map)],
        out_specs=pl.BlockSpec((1, rows, tn), o_map),
        scratch_shapes=[pltpu.VMEM((d, 2 * tn), BF16),
                        pltpu.VMEM((MM_ROWS, 2 * tn), F32), pltpu.VMEM((MM_ROWS, 2 * tn), F32)],
    )
    return pl.pallas_call(
        _moe_up_kernel,
        grid_spec=grid_spec,
        out_shape=jax.ShapeDtypeStruct((n_slots, rows, dff), BF16),
        compiler_params=_params(("arbitrary", "arbitrary")),
        name="moe_up",
    )(experts, nsub, counts, xs, w_gate, w_up, b_gate, b_up)


def _moe_down(experts, nsub, counts, hid, w_down, b_down, tn):
    n_slots, rows, dff = hid.shape
    d = w_down.shape[2]
    nj = d // tn
    x_map, w_map, o_map = _item_maps(nj)
    grid_spec = pltpu.PrefetchScalarGridSpec(
        num_scalar_prefetch=3,
        grid=(n_slots, nj),
        in_specs=[pl.BlockSpec((1, rows, dff), x_map),
                  pl.BlockSpec((1, dff, tn), w_map),
                  pl.BlockSpec((1, 1, tn), w_map)],
        out_specs=pl.BlockSpec((1, rows, tn), o_map),
        scratch_shapes=[pltpu.VMEM((dff, tn), BF16),
                        pltpu.VMEM((MM_ROWS, tn), F32), pltpu.VMEM((MM_ROWS, tn), F32)],
    )
    return pl.pallas_call(
        _moe_down_kernel,
        grid_spec=grid_spec,
        out_shape=jax.ShapeDtypeStruct((n_slots, rows, d), F32),
        compiler_params=_params(("arbitrary", "arbitrary")),
        name="moe_down",
    )(experts, nsub, counts, hid, w_down, b_down)


def _combine_kernel(dest_sm, y_hbm, p_ref, h1_ref, nw_ref, yp_ref, ys_ref, buf, sem, *, n_prompt_tiles):
    i = pl.program_id(0)
    n_steps = pl.num_programs(0)
    n_tok = dest_sm.shape[0] // TOP_K

    def issue(ti, slot):
        for k in range(TOP_K):
            def body(r, carry):
                src = dest_sm[k * n_tok + ti * ROW_TILE + r]
                pltpu.make_async_copy(y_hbm.at[pl.ds(src, 1)], buf.at[slot, k, pl.ds(r, 1)], sem.at[slot]).start()
                return carry

            lax.fori_loop(0, ROW_TILE, body, 0, unroll=8)

    @pl.when(i == 0)
    def _():
        issue(0, 0)

    @pl.when(i + 1 < n_steps)
    def _():
        issue(i + 1, (i + 1) % 2)

    slot = i % 2
    for k in range(TOP_K):
        pltpu.make_async_copy(y_hbm.at[pl.ds(0, ROW_TILE)], buf.at[slot, k], sem.at[slot]).wait()
    group = 16

    def combine_into(out_ref):
        def rows(r, carry):
            r0 = pl.multiple_of(r * group, group)
            acc = h1_ref[pl.ds(r0, group), :]
            for k in range(TOP_K):
                acc = acc + p_ref[pl.ds(r0, group), k:k + 1] * buf[slot, k, pl.ds(r0, group), :]
            out_ref[pl.ds(r0, group), :] = acc * _rms_scale(acc) * nw_ref[...]
            return carry

        lax.fori_loop(0, ROW_TILE // group, rows, 0, unroll=2)

    @pl.when(i < n_prompt_tiles)
    def _():
        combine_into(yp_ref)

    @pl.when(i >= n_prompt_tiles)
    def _():
        combine_into(ys_ref)


def _combine(dest_flat, y_rows, probs, h1, nw, n_prompt, n_sample):
    n, d = h1.shape
    npt = n_prompt // ROW_TILE
    grid_spec = pltpu.PrefetchScalarGridSpec(
        num_scalar_prefetch=1,
        grid=(n // ROW_TILE,),
        in_specs=[pl.BlockSpec(memory_space=pl.ANY),
                  pl.BlockSpec((ROW_TILE, TOP_K), lambda i, *_: (i, 0)),
                  pl.BlockSpec((ROW_TILE, d), lambda i, *_: (i, 0)),
                  pl.BlockSpec((1, d), lambda i, *_: (0, 0))],
        out_specs=[pl.BlockSpec((ROW_TILE, d), lambda i, *_: (jnp.minimum(i, npt - 1), 0)),
                   pl.BlockSpec((ROW_TILE, d), lambda i, *_: (jnp.maximum(i - npt, 0), 0))],
        scratch_shapes=[pltpu.VMEM((2, TOP_K, ROW_TILE, d), F32), pltpu.SemaphoreType.DMA((2,))],
    )
    return pl.pallas_call(
        functools.partial(_combine_kernel, n_prompt_tiles=npt),
        grid_spec=grid_spec,
        out_shape=[jax.ShapeDtypeStruct((n_prompt, d), F32), jax.ShapeDtypeStruct((n_sample, d), F32)],
        compiler_params=_params(("arbitrary",)),
        name="moe_combine",
    )(dest_flat, y_rows, probs, h1, nw)


def _rope_tables(pos0, t, dk):
    inv = 1.0 / (ROPE_BASE ** jnp.linspace(0.0, 1.0, dk // 2, dtype=F32))
    ang = (pos0 + jnp.arange(t, dtype=F32))[:, None] * inv[None, :]
    cos = jnp.repeat(jnp.cos(ang), 2, axis=-1)
    sin = jnp.repeat(jnp.sin(ang), 2, axis=-1)
    sign = jnp.where(jnp.arange(dk) % 2 == 0, -1.0, 1.0).astype(F32)
    return cos, sin * sign


def kernel(x_prompt, x_sample, state_gla, state_ret, norm_mix, w_in, gla_w_gk, gla_b_gk, gla_norm_w, w_out,
           norm_ffn, router_w, router_b, w_gate, b_gate, w_up, b_up, w_down, b_down, norm_final):
    bp, tp, d = x_prompt.shape
    bs, ts, _ = x_sample.shape
    assert ts == 1 and w_in.shape[0] == 1
    dk, dv = state_gla.shape[-2], state_gla.shape[-1]
    qk, vv = HEADS * dk, HEADS * dv
    n_p, n_s = bp * tp, bs * ts
    n = n_p + n_s
    ne = router_w.shape[-1]
    assert n_p % ROW_TILE == 0 and n_s % ROW_TILE == 0

    w = w_in[0]
    c_rank = 2 * qk + 2 * vv
    w_main = jnp.concatenate([w[:, :c_rank], w[:, c_rank + GLA_RANK:]], axis=1).astype(BF16)
    w_rank = jnp.pad(w[:, c_rank:c_rank + GLA_RANK], ((0, 0), (0, LANE - GLA_RANK))).astype(BF16)
    w_gk = jnp.pad(gla_w_gk[0], ((0, LANE - GLA_RANK), (0, 0))).astype(BF16)
    b_gk = gla_b_gk[0][None, :]
    cols = (0, qk, 2 * qk, 2 * qk + vv, c_rank, c_rank + qk, c_rank + 2 * qk, c_rank + 2 * qk + vv)
    nmix = norm_mix[0][None, :]
    wo = w_out[0].astype(BF16)
    wo_g, wo_r = wo[:vv], wo[vv:]
    rwt = router_w[0].T.astype(BF16)
    rb = router_b[0][:, None]
    nffn = norm_ffn[0][None, :]
    gnw = gla_norm_w[0][None, :]

    xp = x_prompt.reshape(n_p, d)
    proj_p, lg_p = _in_proj(xp, nmix, w_main, w_rank, w_gk, b_gk, tm=min(512, n_p), tn=1024)
    cos_p, sin_p = _rope_tables(0.0, tp, dk)
    og_p, st_gla_p = _gla_prompt(proj_p, lg_p, gnw, bp, tp, dk, dv, cols[0], cols[1], cols[2], cols[3])
    or_p, s_ret_p = _ret_prompt(proj_p, cos_p, sin_p, bp, tp, dk, dv, cols[4], cols[5], cols[6], cols[7])
    s_gla_p = jnp.swapaxes(st_gla_p, -1, -2)

    xs = x_sample.reshape(n_s, d)
    proj_s, lg_s = _in_proj(xs, nmix, w_main, w_rank, w_gk, b_gk, tm=n_s, tn=1024)
    cos_s, sin_s = _rope_tables(float(PAST_LEN), 1, dk)
    og_s, or_s, s_gla_s, s_ret_s = _step(proj_s, lg_s, cos_s, sin_s, gnw, state_gla[0], state_ret[0], dk, dv, cols)

    tm_p = min(256, n_p)
    h1, lt_p = _out_proj(og_p, or_p, xp, wo_g, wo_r, nffn, rwt, rb, tm_p, n, 0)
    h1, lt_s = _out_proj(og_s, or_s, xs, wo_g, wo_r, nffn, rwt, rb, n_s, n, n_p // n_s, h1_prev=h1)
    logits_t = jnp.concatenate([lt_p, lt_s], axis=1)

    n_slots = (n * TOP_K) // SLOT_ROWS + ne
    n_tiles_max = (n * TOP_K) // MM_ROWS + ne
    dest, probs_t, item_meta, tile_meta = _route(logits_t, n_tiles_max)
    pad_lo = item_meta[4, :ne]
    pad_hi = item_meta[5, :ne]
    dest_flat = dest.reshape(-1)
    experts = item_meta[0, :n_slots]
    nsub = item_meta[1, :n_slots]
    counts = item_meta[2:4, 0]
    tile_ids = tile_meta[0, :n_tiles_max]

    xs_rows = _gather(dest_flat, tile_ids, counts, pad_lo, pad_hi, h1, nffn, n_slots, n_tiles_max)
    hid = _moe_up(experts, nsub, counts, xs_rows.reshape(n_slots, SLOT_ROWS, d), w_gate[0], w_up[0],
                  b_gate[0][:, None, :], b_up[0][:, None, :], tn=256)
    y_rows = _moe_down(experts, nsub, counts, hid, w_down[0], b_down[0][:, None, :], tn=512)

    y_p, y_s = _combine(dest_flat, y_rows.reshape(n_slots * SLOT_ROWS, d), probs_t.T, h1,
                        norm_final[None, :], n_p, n_s)
    return (y_p.reshape(bp, tp, d), y_s.reshape(bs, ts, d),
            s_gla_p[None], s_ret_p[None], s_gla_s[None], s_ret_s[None])
```

```python
import functools
import math

import numpy as np
import jax
import jax.numpy as jnp
from jax import lax
from jax.experimental import pallas as pl
from jax.experimental.pallas import tpu as pltpu

F32 = jnp.float32
BF16 = jnp.bfloat16
I32 = jnp.int32

HEADS = 4
GLA_RANK = 16
GLA_GATE_NORM = 16.0
ROPE_BASE = 10000.0
TOP_K = 4
SWIGLU_LIMIT = 7.0
SWIGLU_ALPHA = 1.702
EPS = 1e-6
PAST_LEN = 16384

LANE = 128
ROW_TILE = 128
SLOT_TILES = 10
SLOT_ROWS = SLOT_TILES * ROW_TILE
MM_ROWS = 2 * ROW_TILE
SLOT_CHUNKS = SLOT_ROWS // MM_ROWS
GLA_CHUNK = 64
GLA_SUB = 16
RET_CHUNK = 256
VMEM_LIMIT = 60 * 1024 * 1024

_NT = (((1,), (1,)), ((), ()))
_TN = (((0,), (0,)), ((), ()))


def _params(sem, vmem=VMEM_LIMIT):
    return pltpu.CompilerParams(dimension_semantics=sem, vmem_limit_bytes=vmem)


def _sigmoid(x):
    return 1.0 / (1.0 + jnp.exp(-x))


def _rms_scale(x):
    return lax.rsqrt(jnp.mean(x * x, axis=-1, keepdims=True) + EPS)


def _in_proj_kernel(x_ref, nw_ref, w_ref, wr_ref, wgk_ref, bgk_ref, proj_ref, lg_ref, h_scr):
    @pl.when(pl.program_id(1) == 0)
    def _():
        x = x_ref[...]
        hb = (x * _rms_scale(x) * nw_ref[...]).astype(BF16)
        h_scr[...] = hb
        gr = jnp.dot(hb, wr_ref[...], preferred_element_type=F32)
        z = jnp.dot(gr.astype(BF16), wgk_ref[...], preferred_element_type=F32) + bgk_ref[...]
        log_sig = jnp.minimum(z, 0.0) - jnp.log1p(jnp.exp(-jnp.abs(z)))
        lg_ref[...] = log_sig * (1.0 / GLA_GATE_NORM)

    proj_ref[...] = jnp.dot(h_scr[...], w_ref[...], preferred_element_type=F32)


def _in_proj(x, nw, w, wr, wgk, bgk, tm, tn):
    m, d = x.shape
    n = w.shape[1]
    assert m % tm == 0 and n % tn == 0
    nlg = wgk.shape[1]
    return pl.pallas_call(
        _in_proj_kernel,
        grid=(m // tm, n // tn),
        in_specs=[
            pl.BlockSpec((tm, d), lambda i, j: (i, 0)),
            pl.BlockSpec((1, d), lambda i, j: (0, 0)),
            pl.BlockSpec((d, tn), lambda i, j: (0, j)),
            pl.BlockSpec((d, LANE), lambda i, j: (0, 0)),
            pl.BlockSpec((LANE, nlg), lambda i, j: (0, 0)),
            pl.BlockSpec((1, nlg), lambda i, j: (0, 0)),
        ],
        out_specs=[
            pl.BlockSpec((tm, tn), lambda i, j: (i, j)),
            pl.BlockSpec((tm, nlg), lambda i, j: (i, 0)),
        ],
        out_shape=[jax.ShapeDtypeStruct((m, n), F32), jax.ShapeDtypeStruct((m, nlg), F32)],
        scratch_shapes=[pltpu.VMEM((tm, d), BF16)],
        compiler_params=_params(("parallel", "arbitrary")),
        name="in_proj",
    )(x, nw, w, wr, wgk, bgk)


def _rope(x, cos, sin_signed, even):
    n = x.shape[-1]
    nxt = pltpu.roll(x, n - 1, axis=x.ndim - 1)
    prv = pltpu.roll(x, 1, axis=x.ndim - 1)
    return x * cos + jnp.where(even, nxt, prv) * sin_signed


def _ret_prompt_kernel(q_ref, k_ref, v_ref, g_ref, cos_ref, sin_ref, o_ref, s_ref):
    c = q_ref.shape[0]
    dk = q_ref.shape[1] // HEADS
    dv = v_ref.shape[1] // HEADS

    @pl.when(pl.program_id(1) == 0)
    def _():
        s_ref[...] = jnp.zeros_like(s_ref)

    ti = lax.broadcasted_iota(I32, (c, c), 0)
    si = lax.broadcasted_iota(I32, (c, c), 1)
    dlt = (ti - si).astype(F32)
    causal = ti >= si
    tcol = lax.broadcasted_iota(I32, (c, 1), 0).astype(F32)
    even = (lax.broadcasted_iota(I32, (c, dk), 1) % 2) == 0
    cos = cos_ref[...]
    sin = sin_ref[...]
    for h in range(HEADS):
        lgam = math.log(1.0 - 2.0 ** (-5.0 - h))
        q = _rope(q_ref[:, h * dk:(h + 1) * dk], cos, sin, even)
        k = _rope(k_ref[:, h * dk:(h + 1) * dk], cos, sin, even) * (dk ** -0.5)
        qb = q.astype(BF16)
        vb = v_ref[:, h * dv:(h + 1) * dv].astype(BF16)
        decay = jnp.where(causal, jnp.exp(dlt * lgam), 0.0)
        a = lax.dot_general(qb, k.astype(BF16), _NT, preferred_element_type=F32) * decay
        s = s_ref[0, h]
        o = jnp.dot(a.astype(BF16), vb, preferred_element_type=F32)
        o = o + jnp.dot(qb, s.astype(BF16), preferred_element_type=F32) * jnp.exp((tcol + 1.0) * lgam)
        kd = (k * jnp.exp((c - 1.0 - tcol) * lgam)).astype(BF16)
        s_ref[0, h] = math.exp(c * lgam) * s + lax.dot_general(kd, vb, _TN, preferred_element_type=F32)
        g = g_ref[:, h * dv:(h + 1) * dv]
        o_ref[:, h * dv:(h + 1) * dv] = (o * _rms_scale(o) * (g * _sigmoid(g))).astype(BF16)


def _ret_prompt(proj, cos, sin, batch, seq, dk, dv, col_q, col_k, col_v, col_g):
    c = math.gcd(seq, RET_CHUNK)
    nt = seq // c
    qw, vw = HEADS * dk, HEADS * dv
    return pl.pallas_call(
        _ret_prompt_kernel,
        grid=(batch, nt),
        in_specs=[
            pl.BlockSpec((c, qw), lambda b, t: (b * nt + t, col_q // qw)),
            pl.BlockSpec((c, qw), lambda b, t: (b * nt + t, col_k // qw)),
            pl.BlockSpec((c, vw), lambda b, t: (b * nt + t, col_v // vw)),
            pl.BlockSpec((c, vw), lambda b, t: (b * nt + t, col_g // vw)),
            pl.BlockSpec((c, dk), lambda b, t: (t, 0)),
            pl.BlockSpec((c, dk), lambda b, t: (t, 0)),
        ],
        out_specs=[
            pl.BlockSpec((c, vw), lambda b, t: (b * nt + t, 0)),
            pl.BlockSpec((1, HEADS, dk, dv), lambda b, t: (b, 0, 0, 0)),
        ],
        out_shape=[jax.ShapeDtypeStruct((batch * seq, vw), BF16),
                   jax.ShapeDtypeStruct((batch, HEADS, dk, dv), F32)],
        compiler_params=_params(("parallel", "arbitrary")),
        name="ret_prompt",
    )(proj, proj, proj, proj, cos, sin)


def _gla_prompt_kernel(q_ref, k_ref, v_ref, g_ref, lg_ref, nw_ref, o_ref, st_ref, b_scr, a_scr):
    c = q_ref.shape[0]
    dk = q_ref.shape[1] // HEADS
    dv = v_ref.shape[1] // HEADS
    sub = GLA_SUB
    nsub = c // sub

    @pl.when(pl.program_id(1) == 0)
    def _():
        st_ref[...] = jnp.zeros_like(st_ref)

    ri = lax.broadcasted_iota(I32, (c, c), 0)
    ci = lax.broadcasted_iota(I32, (c, c), 1)
    lower = (ri >= ci).astype(BF16)
    cis = lax.broadcasted_iota(I32, (sub, c), 1)

    for h in range(HEADS):
        lg = lg_ref[:, h * dk:(h + 1) * dk]
        l1 = lg.astype(BF16)
        r1 = lg - l1.astype(F32)
        l2 = r1.astype(BF16)
        l3 = (r1 - l2.astype(F32)).astype(BF16)
        b_scr[h] = (jnp.dot(lower, l1, preferred_element_type=F32)
                    + jnp.dot(lower, l2, preferred_element_type=F32)
                    + jnp.dot(lower, l3, preferred_element_type=F32))

    def sub_block(i, carry):
        r0 = pl.multiple_of(i * sub, sub)
        rprev = jnp.maximum(r0 - 1, 0)
        for h in range(HEADS):
            hs = slice(h * dk, (h + 1) * dk)
            qi = q_ref[pl.ds(r0, sub), hs] * (dk ** -0.5)
            ki = k_ref[pl.ds(r0, sub), hs]
            bi = b_scr[h, pl.ds(r0, sub), :]
            ad = jnp.zeros((sub, c), F32)
            for s in range(sub):
                e = jnp.exp(jnp.minimum(bi - bi[s:s + 1, :], 0.0))
                col = jnp.sum(qi * e * ki[s:s + 1, :], axis=1, keepdims=True)
                ad = jnp.where(cis == r0 + s, col, ad)
            bref = b_scr[h, pl.ds(rprev, 1), :]
            qq = (qi * jnp.exp(jnp.minimum(bi - bref, 0.0))).astype(BF16)
            kk = (k_ref[:, hs] * jnp.exp(jnp.minimum(bref - b_scr[h], 0.0))).astype(BF16)
            ao = lax.dot_general(qq, kk, _NT, preferred_element_type=F32)
            a_scr[h, pl.ds(r0, sub), :] = jnp.where(cis < r0, ao, ad)
        return carry

    lax.fori_loop(0, nsub, sub_block, 0)

    for h in range(HEADS):
        hs = slice(h * dk, (h + 1) * dk)
        vs = slice(h * dv, (h + 1) * dv)
        b = b_scr[h]
        b_last = b[c - 1:c, :]
        vb = v_ref[:, vs].astype(BF16)
        qe = (q_ref[:, hs] * (dk ** -0.5) * jnp.exp(b)).astype(BF16)
        kd = (k_ref[:, hs] * jnp.exp(b_last - b)).astype(BF16)
        st = st_ref[0, h]
        a = jnp.where(ri >= ci, a_scr[h], 0.0)
        o = jnp.dot(a.astype(BF16), vb, preferred_element_type=F32)
        o = o + lax.dot_general(qe, st.astype(BF16), _NT, preferred_element_type=F32)
        st_ref[0, h] = st * jnp.exp(b_last) + lax.dot_general(vb, kd, _TN, preferred_element_type=F32)
        g = g_ref[:, vs]
        o_ref[:, vs] = (o * _rms_scale(o) * nw_ref[...] * (g * _sigmoid(g))).astype(BF16)


def _gla_prompt(proj, lg, nw, batch, seq, dk, dv, col_q, col_k, col_v, col_g):
    c = math.gcd(seq, GLA_CHUNK)
    nt = seq // c
    qw, vw = HEADS * dk, HEADS * dv
    return pl.pallas_call(
        _gla_prompt_kernel,
        grid=(batch, nt),
        in_specs=[
            pl.BlockSpec((c, qw), lambda b, t: (b * nt + t, col_q // qw)),
            pl.BlockSpec((c, qw), lambda b, t: (b * nt + t, col_k // qw)),
            pl.BlockSpec((c, vw), lambda b, t: (b * nt + t, col_v // vw)),
            pl.BlockSpec((c, vw), lambda b, t: (b * nt + t, col_g // vw)),
            pl.BlockSpec((c, qw), lambda b, t: (b * nt + t, 0)),
            pl.BlockSpec((1, dv), lambda b, t: (0, 0)),
        ],
        out_specs=[
            pl.BlockSpec((c, vw), lambda b, t: (b * nt + t, 0)),
            pl.BlockSpec((1, HEADS, dv, dk), lambda b, t: (b, 0, 0, 0)),
        ],
        out_shape=[jax.ShapeDtypeStruct((batch * seq, vw), BF16),
                   jax.ShapeDtypeStruct((batch, HEADS, dv, dk), F32)],
        scratch_shapes=[pltpu.VMEM((HEADS, c, dk), F32), pltpu.VMEM((HEADS, c, c), F32)],
        compiler_params=_params(("parallel", "arbitrary")),
        name="gla_prompt",
    )(proj, proj, proj, proj, lg, nw)


def _to_col(row, eye):
    return jnp.sum(jnp.where(eye, row, 0.0), axis=1, keepdims=True)


def _step_kernel(proj_ref, lg_ref, cos_ref, sin_ref, nw_ref, sg_ref, sr_ref,
                 og_ref, or_ref, ng_ref, nr_ref, *, dk, dv, cols):
    b = pl.program_id(0)
    col_gq, col_gk, col_gv, col_gg, col_rq, col_rk, col_rv, col_rg = cols
    eye = lax.broadcasted_iota(I32, (dk, dk), 0) == lax.broadcasted_iota(I32, (dk, dk), 1)
    even = (lax.broadcasted_iota(I32, (1, dk), 1) % 2) == 0
    row = pl.ds(b, 1)

    def finish(o, g, w):
        o = o * _rms_scale(o)
        if w is not None:
            o = o * w
        return o * (g * _sigmoid(g))

    for h in range(HEADS):
        q = proj_ref[row, col_gq + h * dk:col_gq + (h + 1) * dk] * (dk ** -0.5)
        k = proj_ref[row, col_gk + h * dk:col_gk + (h + 1) * dk]
        v = proj_ref[row, col_gv + h * dv:col_gv + (h + 1) * dv]
        g = proj_ref[row, col_gg + h * dv:col_gg + (h + 1) * dv]
        a = jnp.exp(lg_ref[row, h * dk:(h + 1) * dk])
        s_new = _to_col(a, eye) * sg_ref[0, h] + _to_col(k, eye) * v
        ng_ref[0, h] = s_new
        o = jnp.sum(_to_col(q, eye) * s_new, axis=0, keepdims=True)
        og_ref[row, h * dv:(h + 1) * dv] = finish(o, g, nw_ref[...])

        gamma = 1.0 - 2.0 ** (-5.0 - h)
        q = _rope(proj_ref[row, col_rq + h * dk:col_rq + (h + 1) * dk], cos_ref[...], sin_ref[...], even)
        k = _rope(proj_ref[row, col_rk + h * dk:col_rk + (h + 1) * dk], cos_ref[...], sin_ref[...], even)
        k = k * (dk ** -0.5)
        v = proj_ref[row, col_rv + h * dv:col_rv + (h + 1) * dv]
        g = proj_ref[row, col_rg + h * dv:col_rg + (h + 1) * dv]
        s_new = gamma * sr_ref[0, h] + _to_col(k, eye) * v
        nr_ref[0, h] = s_new
        o = jnp.sum(_to_col(q, eye) * s_new, axis=0, keepdims=True)
        or_ref[row, h * dv:(h + 1) * dv] = finish(o, g, None)


def _step(proj, lg, cos, sin, nw, s_gla, s_ret, dk, dv, cols):
    nb, ncol = proj.shape
    vw = HEADS * dv
    state_spec = pl.BlockSpec((1, HEADS, dk, dv), lambda b: (b, 0, 0, 0))
    full = lambda shape: pl.BlockSpec(shape, lambda b: (0,) * len(shape))
    return pl.pallas_call(
        functools.partial(_step_kernel, dk=dk, dv=dv, cols=cols),
        grid=(nb,),
        in_specs=[full((nb, ncol)), full(lg.shape), full((1, dk)), full((1, dk)), full((1, dv)),
                  state_spec, state_spec],
        out_specs=[full((nb, vw)), full((nb, vw)), state_spec, state_spec],
        out_shape=[jax.ShapeDtypeStruct((nb, vw), F32), jax.ShapeDtypeStruct((nb, vw), F32),
                   jax.ShapeDtypeStruct(s_gla.shape, F32), jax.ShapeDtypeStruct(s_ret.shape, F32)],
        compiler_params=_params(("arbitrary",)),
        name="sample_step",
    )(proj, lg, cos, sin, nw, s_gla, s_ret)


def _out_proj_kernel(og_ref, or_ref, x_ref, wg_ref, wr_ref, nw_ref, rwt_ref, rb_ref, *rest):
    h1_ref, lt_ref = rest[-2], rest[-1]
    acc = jnp.dot(og_ref[...].astype(BF16), wg_ref[...], preferred_element_type=F32)
    acc = acc + jnp.dot(or_ref[...].astype(BF16), wr_ref[...], preferred_element_type=F32)
    h1 = x_ref[...] + acc
    h1_ref[...] = h1
    flat = (h1 * _rms_scale(h1) * nw_ref[...]).astype(BF16)
    lt_ref[...] = lax.dot_general(rwt_ref[...], flat, _NT, preferred_element_type=F32) + rb_ref[...]


def _out_proj(og, orr, x, wg, wr, nw, rwt, rb, tm, n_total, row_block0, h1_prev=None):
    m, d = x.shape
    half = og.shape[1]
    ne = rwt.shape[0]
    const = lambda shape: pl.BlockSpec(shape, lambda i: (0,) * len(shape), pipeline_mode=pl.Buffered(1))
    in_specs = [
        pl.BlockSpec((tm, half), lambda i: (i, 0)),
        pl.BlockSpec((tm, half), lambda i: (i, 0)),
        pl.BlockSpec((tm, d), lambda i: (i, 0)),
        const((half, d)), const((half, d)), const((1, d)), const((ne, d)), const((ne, 1)),
    ]
    args = [og, orr, x, wg, wr, nw, rwt, rb]
    aliases = {}
    if h1_prev is not None:
        in_specs.append(pl.BlockSpec(memory_space=pl.ANY))
        args.append(h1_prev)
        aliases = {len(args) - 1: 0}
    return pl.pallas_call(
        _out_proj_kernel,
        grid=(m // tm,),
        in_specs=in_specs,
        out_specs=[
            pl.BlockSpec((tm, d), lambda i: (row_block0 + i, 0)),
            pl.BlockSpec((ne, tm), lambda i: (0, i)),
        ],
        out_shape=[jax.ShapeDtypeStruct((n_total, d), F32), jax.ShapeDtypeStruct((ne, m), F32)],
        input_output_aliases=aliases,
        compiler_params=_params(("parallel",)),
        name="out_proj",
    )(*args)


def _route_kernel(lt_ref, dest_ref, prob_ref, item_ref, tile_ref, m_scr, pos_scr):
    ne, n = lt_ref.shape
    nblk = n // LANE
    logits = lt_ref[...]
    eio = lax.broadcasted_iota(I32, (ne, n), 0).astype(F32)
    vals, idxs = [], []
    for _ in range(TOP_K):
        m = jnp.max(logits, axis=0, keepdims=True)
        ik = jnp.min(jnp.where(logits == m, eio, float(ne)), axis=0, keepdims=True)
        vals.append(m)
        idxs.append(ik)
        logits = jnp.where(eio == ik, -jnp.inf, logits)
    ex = [jnp.exp(v - vals[0]) for v in vals]
    den = ex[0] + ex[1] + ex[2] + ex[3]
    for k in range(TOP_K):
        prob_ref[k:k + 1, :] = ex[k] / den
    sel = jnp.zeros((ne, n), F32)
    for ik in idxs:
        sel = jnp.where(eio == ik, 1.0, sel)
    m_scr[...] = sel

    upper = (lax.broadcasted_iota(I32, (LANE, LANE), 0) <= lax.broadcasted_iota(I32, (LANE, LANE), 1)).astype(BF16)

    def prefix(cb, carry):
        c0 = pl.multiple_of(cb * LANE, LANE)
        mc = m_scr[:, pl.ds(c0, LANE)]
        incl = jnp.dot(mc.astype(BF16), upper, preferred_element_type=F32)
        pos_scr[:, pl.ds(c0, LANE)] = incl - mc + carry
        return carry + jnp.sum(mc, axis=1, keepdims=True)

    cnt = lax.fori_loop(0, nblk, prefix, jnp.zeros((ne, 1), F32))

    def ceil_div(x, step, most):
        r = jnp.zeros_like(x)
        for mlt in range(most):
            r = r + jnp.where(x > float(mlt * step), 1.0, 0.0)
        return r

    cnt_l = jnp.broadcast_to(cnt, (ne, LANE))
    n_slot = ceil_div(cnt_l, SLOT_ROWS, -(-n // SLOT_ROWS))
    n_tile = ceil_div(cnt_l, MM_ROWS, -(-n // MM_ROWS))
    strict = (lax.broadcasted_iota(I32, (ne, ne), 0) > lax.broadcasted_iota(I32, (ne, ne), 1)).astype(BF16)
    slot0 = jnp.dot(strict, n_slot.astype(BF16), preferred_element_type=F32)
    tile0 = jnp.dot(strict, n_tile.astype(BF16), preferred_element_type=F32)

    destf = slot0[:, :1] * float(SLOT_ROWS) + pos_scr[...]
    for k in range(TOP_K):
        dk_ = jnp.sum(jnp.where(eio == idxs[k], destf, 0.0), axis=0, keepdims=True)
        dest_ref[k:k + 1, :] = dk_.astype(I32)

    def lookup(table, onehot):
        return jnp.sum(jnp.where(onehot, table, 0.0), axis=0, keepdims=True)

    e_col = lax.broadcasted_iota(I32, (ne, LANE), 0).astype(F32)
    it = lax.broadcasted_iota(I32, (ne, LANE), 1).astype(F32)
    n_items = jnp.sum(n_slot, axis=0, keepdims=True)
    e_it = jnp.minimum(jnp.sum(jnp.where(slot0 + n_slot <= it, 1.0, 0.0), axis=0, keepdims=True), ne - 1.0)
    oh = e_col == e_it
    rows = lookup(cnt_l, oh) - (it[:1] - lookup(slot0, oh)) * float(SLOT_ROWS)
    rows = jnp.clip(rows, 0.0, float(SLOT_ROWS))
    rows = jnp.where(it[:1] < n_items, rows, 0.0)
    item_ref[...] = jnp.zeros_like(item_ref)
    item_ref[0:1, :] = e_it.astype(I32)
    item_ref[1:2, :] = ceil_div(rows, ROW_TILE, SLOT_TILES).astype(I32)
    item_ref[2:3, :] = n_items.astype(I32)
    item_ref[3:4, :] = jnp.sum(n_tile, axis=0, keepdims=True).astype(I32)
    diag = e_col == it
    item_ref[4:5, :] = lookup(slot0 * float(SLOT_ROWS) + cnt_l, diag).astype(I32)
    item_ref[5:6, :] = lookup(slot0 * float(SLOT_ROWS) + n_tile * float(MM_ROWS), diag).astype(I32)

    ntl = tile_ref.shape[1]
    tile_ref[...] = jnp.zeros_like(tile_ref)
    for cb in range(ntl // LANE):
        gi = it + float(cb * LANE)
        e_g = jnp.minimum(jnp.sum(jnp.where(tile0 + n_tile <= gi, 1.0, 0.0), axis=0, keepdims=True), ne - 1.0)
        ohg = e_col == e_g
        tid = lookup(slot0, ohg) * float(SLOT_CHUNKS) + (gi[:1] - lookup(tile0, ohg))
        tile_ref[0:1, cb * LANE:(cb + 1) * LANE] = tid.astype(I32)


def _route(logits_t, n_tiles_max):
    ne, n = logits_t.shape
    ntl = -(-n_tiles_max // LANE) * LANE
    return pl.pallas_call(
        _route_kernel,
        out_shape=[jax.ShapeDtypeStruct((TOP_K, n), I32), jax.ShapeDtypeStruct((TOP_K, n), F32),
                   jax.ShapeDtypeStruct((8, LANE), I32), jax.ShapeDtypeStruct((8, ntl), I32)],
        scratch_shapes=[pltpu.VMEM((ne, n), F32), pltpu.VMEM((ne, n), F32)],
        compiler_params=pltpu.CompilerParams(vmem_limit_bytes=VMEM_LIMIT),
        name="route",
    )(logits_t)


def _gather_kernel(dest_sm, tile_sm, cnt_sm, pad_lo_sm, pad_hi_sm, h1_hbm, nw_ref, xs_ref, inv_sm, buf, sem):
    g = pl.program_id(0)
    n_tiles = cnt_sm[1]
    n_tok = dest_sm.shape[0] // TOP_K
    group = 16

    def issue(gi, slot):
        base = tile_sm[gi] * MM_ROWS

        def body(r, carry):
            src = inv_sm[base + r]
            pltpu.make_async_copy(h1_hbm.at[pl.ds(src, 1)], buf.at[slot, pl.ds(r, 1)], sem.at[slot]).start()
            return carry

        lax.fori_loop(0, MM_ROWS, body, 0, unroll=8)

    @pl.when(g == 0)
    def _():
        spread = (1 << (n_tok.bit_length() - 1)) - 1

        def pad_expert(e, carry):
            def clear(i, c2):
                inv_sm[i] = i & spread
                return c2

            lax.fori_loop(pad_lo_sm[e], pad_hi_sm[e], clear, 0)
            return carry

        lax.fori_loop(0, pad_lo_sm.shape[0], pad_expert, 0)
        for k in range(TOP_K):
            def scatter(t, carry):
                inv_sm[dest_sm[k * n_tok + t]] = t
                return carry

            lax.fori_loop(0, n_tok, scatter, 0, unroll=8)
        issue(0, 0)

    @pl.when(g + 1 < n_tiles)
    def _():
        issue(g + 1, (g + 1) % 2)

    @pl.when(g < n_tiles)
    def _():
        slot = g % 2
        pltpu.make_async_copy(h1_hbm.at[pl.ds(0, MM_ROWS)], buf.at[slot], sem.at[slot]).wait()

        def norm_rows(r, carry):
            r0 = pl.multiple_of(r * group, group)
            x = buf[slot, pl.ds(r0, group), :]
            xs_ref[pl.ds(r0, group), :] = (x * _rms_scale(x) * nw_ref[...]).astype(BF16)
            return carry

        lax.fori_loop(0, MM_ROWS // group, norm_rows, 0, unroll=2)


def _gather(dest_flat, tile_ids, counts, pad_lo, pad_hi, h1, nw, n_slots, n_tiles_max):
    d = h1.shape[1]

    def out_map(g, dest_sm, tile_sm, cnt_sm, *_):
        return (tile_sm[jnp.minimum(g, cnt_sm[1] - 1)], 0)

    grid_spec = pltpu.PrefetchScalarGridSpec(
        num_scalar_prefetch=5,
        grid=(n_tiles_max,),
        in_specs=[pl.BlockSpec(memory_space=pl.ANY),
                  pl.BlockSpec((1, d), lambda g, *_: (0, 0))],
        out_specs=pl.BlockSpec((MM_ROWS, d), out_map),
        scratch_shapes=[pltpu.SMEM((n_slots * SLOT_ROWS,), I32),
                        pltpu.VMEM((2, MM_ROWS, d), F32),
                        pltpu.SemaphoreType.DMA((2,))],
    )
    return pl.pallas_call(
        _gather_kernel,
        grid_spec=grid_spec,
        out_shape=jax.ShapeDtypeStruct((n_slots * SLOT_ROWS, d), BF16),
        compiler_params=_params(("arbitrary",)),
        name="moe_gather",
    )(dest_flat, tile_ids, counts, pad_lo, pad_hi, h1, nw)


def _for_row_chunks(ns, matmul, finish, clear, acc_scr):
    acc_a, acc_b = acc_scr
    n_full = jnp.maximum(lax.shift_right_logical(ns, 1), 1)
    n_pair = lax.shift_right_logical(n_full - 1, 1)
    row0 = lambda c: pl.multiple_of(c * MM_ROWS, MM_ROWS)
    acc_a[...] = matmul(0, MM_ROWS)

    def pair(p, carry):
        c = 2 * p
        acc_b[...] = matmul(row0(c + 1), MM_ROWS)
        finish(row0(c), MM_ROWS, acc_a[...])
        acc_a[...] = matmul(row0(c + 2), MM_ROWS)
        finish(row0(c + 1), MM_ROWS, acc_b[...])
        return carry

    lax.fori_loop(0, n_pair, pair, 0)
    done = 2 * n_pair
    extra = (n_full - 1) > done
    odd = jnp.logical_and(ns > 1, (ns & 1) == 1)
    t0 = pl.multiple_of((ns - 1) * ROW_TILE, ROW_TILE)

    @pl.when(extra)
    def _():
        acc_b[...] = matmul(row0(done + 1), MM_ROWS)
        finish(row0(done), MM_ROWS, acc_a[...])

    def drain(acc, c):
        @pl.when(odd)
        def _():
            tail = matmul(t0, ROW_TILE)
            finish(row0(c), MM_ROWS, acc[...])
            finish(t0, ROW_TILE, tail)

        @pl.when(jnp.logical_not(odd))
        def _():
            finish(row0(c), MM_ROWS, acc[...])

    @pl.when(extra)
    def _():
        drain(acc_b, done + 1)

    @pl.when(jnp.logical_not(extra))
    def _():
        drain(acc_a, done)

    def clear_tile(s, carry):
        clear(pl.multiple_of(s * ROW_TILE, ROW_TILE))
        return carry

    lax.fori_loop(jnp.maximum(ns, MM_ROWS // ROW_TILE), SLOT_TILES, clear_tile, 0)


def _moe_up_kernel(exp_sm, nsub_sm, cnt_sm, x_ref, wg_ref, wu_ref, bg_ref, bu_ref, hid_ref, w_scr, *acc_scr):
    it = pl.program_id(0)
    tn = wg_ref.shape[2]

    @pl.when(it < cnt_sm[0])
    def _():
        w_scr[:, :tn] = wg_ref[0].astype(BF16)
        w_scr[:, tn:] = wu_ref[0].astype(BF16)

        def matmul(r0, rows):
            return jnp.dot(x_ref[0, pl.ds(r0, rows), :], w_scr[...], preferred_element_type=F32)

        def finish(r0, rows, gu):
            g = jnp.minimum(gu[:, :tn] + bg_ref[0], SWIGLU_LIMIT)
            u = jnp.clip(gu[:, tn:] + bu_ref[0], -SWIGLU_LIMIT, SWIGLU_LIMIT)
            hid = (u + 1.0) * g * _sigmoid(SWIGLU_ALPHA * g)
            hid_ref[0, pl.ds(r0, rows), :] = hid.astype(BF16)

        def clear(r0):
            hid_ref[0, pl.ds(r0, ROW_TILE), :] = jnp.zeros((ROW_TILE, tn), BF16)

        _for_row_chunks(nsub_sm[it], matmul, finish, clear, acc_scr)


def _moe_down_kernel(exp_sm, nsub_sm, cnt_sm, h_ref, w_ref, b_ref, y_ref, w_scr, *acc_scr):
    it = pl.program_id(0)
    tn = w_ref.shape[2]

    @pl.when(it < cnt_sm[0])
    def _():
        w_scr[...] = w_ref[0].astype(BF16)

        def matmul(r0, rows):
            return jnp.dot(h_ref[0, pl.ds(r0, rows), :], w_scr[...], preferred_element_type=F32)

        def finish(r0, rows, y):
            y_ref[0, pl.ds(r0, rows), :] = y + b_ref[0]

        def clear(r0):
            y_ref[0, pl.ds(r0, ROW_TILE), :] = jnp.zeros((ROW_TILE, tn), F32)

        _for_row_chunks(nsub_sm[it], matmul, finish, clear, acc_scr)


def _item_maps(nj):
    def eff(it, j, cnt_sm):
        last = cnt_sm[0] - 1
        return jnp.minimum(it, last), jnp.where(it <= last, j, nj - 1)

    def x_map(it, j, exp_sm, nsub_sm, cnt_sm):
        ie, _ = eff(it, j, cnt_sm)
        return (ie, 0, 0)

    def w_map(it, j, exp_sm, nsub_sm, cnt_sm):
        ie, je = eff(it, j, cnt_sm)
        return (exp_sm[ie], 0, je)

    def o_map(it, j, exp_sm, nsub_sm, cnt_sm):
        ie, je = eff(it, j, cnt_sm)
        return (ie, 0, je)

    return x_map, w_map, o_map


def _moe_up(experts, nsub, counts, xs, w_gate, w_up, b_gate, b_up, tn):
    n_slots, rows, d = xs.shape
    dff = w_gate.shape[2]
    nj = dff // tn
    x_map, w_map, o_map = _item_maps(nj)
    grid_spec = pltpu.PrefetchScalarGridSpec(
        num_scalar_prefetch=3,
        grid=(counts[0], nj),
        in_specs=[pl.BlockSpec((1, rows, d), x_map),
                  pl.BlockSpec((1, d, tn), w_map), pl.BlockSpec((1, d, tn), w_map),
                  pl.BlockSpec((1, 1, tn), w_map), pl.BlockSpec((1, 1, tn), w_map)],
        out_specs=pl.BlockSpec((1, rows, tn), o_map),
        scratch_shapes=[pltpu.VMEM((d, 2 * tn), BF16),
                        pltpu.VMEM((MM_ROWS, 2 * tn), F32), pltpu.VMEM((MM_ROWS, 2 * tn), F32)],
    )
    return pl.pallas_call(
        _moe_up_kernel,
        grid_spec=grid_spec,
        out_shape=jax.ShapeDtypeStruct((n_slots, rows, dff), BF16),
        compiler_params=_params(("arbitrary", "arbitrary")),
        name="moe_up",
    )(experts, nsub, counts, xs, w_gate, w_up, b_gate, b_up)


def _moe_down(experts, nsub, counts, hid, w_down, b_down, tn):
    n_slots, rows, dff = hid.shape
    d = w_down.shape[2]
    nj = d // tn
    x_map, w_map, o_map = _item_maps(nj)
    grid_spec = pltpu.PrefetchScalarGridSpec(
        num_scalar_prefetch=3,
        grid=(counts[0], nj),
        in_specs=[pl.BlockSpec((1, rows, dff), x_map),
                  pl.BlockSpec((1, dff, tn), w_map),
                  pl.BlockSpec((1, 1, tn), w_map)],
        out_specs=pl.BlockSpec((1, rows, tn), o_map),
        scratch_shapes=[pltpu.VMEM((dff, tn), BF16),
                        pltpu.VMEM((MM_ROWS, tn), F32), pltpu.VMEM((MM_ROWS, tn), F32)],
    )
    return pl.pallas_call(
        _moe_down_kernel,
        grid_spec=grid_spec,
        out_shape=jax.ShapeDtypeStruct((n_slots, rows, d), F32),
        compiler_params=_params(("arbitrary", "arbitrary")),
        name="moe_down",
    )(experts, nsub, counts, hid, w_down, b_down)


def _combine_kernel(dest_sm, y_hbm, p_ref, h1_ref, nw_ref, yp_ref, ys_ref, buf, sem, *, n_prompt_tiles):
    i = pl.program_id(0)
    n_steps = pl.num_programs(0)
    n_tok = dest_sm.shape[0] // TOP_K

    def issue(ti, slot):
        for k in range(TOP_K):
            def body(r, carry):
                src = dest_sm[k * n_tok + ti * ROW_TILE + r]
                pltpu.make_async_copy(y_hbm.at[pl.ds(src, 1)], buf.at[slot, k, pl.ds(r, 1)], sem.at[slot]).start()
                return carry

            lax.fori_loop(0, ROW_TILE, body, 0, unroll=8)

    @pl.when(i == 0)
    def _():
        issue(0, 0)

    @pl.when(i + 1 < n_steps)
    def _():
        issue(i + 1, (i + 1) % 2)

    slot = i % 2
    for k in range(TOP_K):
        pltpu.make_async_copy(y_hbm.at[pl.ds(0, ROW_TILE)], buf.at[slot, k], sem.at[slot]).wait()
    group = 16

    def combine_into(out_ref):
        def rows(r, carry):
            r0 = pl.multiple_of(r * group, group)
            acc = h1_ref[pl.ds(r0, group), :]
            for k in range(TOP_K):
                acc = acc + p_ref[pl.ds(r0, group), k:k + 1] * buf[slot, k, pl.ds(r0, group), :]
            out_ref[pl.ds(r0, group), :] = acc * _rms_scale(acc) * nw_ref[...]
            return carry

        lax.fori_loop(0, ROW_TILE // group, rows, 0, unroll=2)

    @pl.when(i < n_prompt_tiles)
    def _():
        combine_into(yp_ref)

    @pl.when(i >= n_prompt_tiles)
    def _():
        combine_into(ys_ref)


def _combine(dest_flat, y_rows, probs, h1, nw, n_prompt, n_sample):
    n, d = h1.shape
    npt = n_prompt // ROW_TILE
    grid_spec = pltpu.PrefetchScalarGridSpec(
        num_scalar_prefetch=1,
        grid=(n // ROW_TILE,),
        in_specs=[pl.BlockSpec(memory_space=pl.ANY),
                  pl.BlockSpec((ROW_TILE, TOP_K), lambda i, *_: (i, 0)),
                  pl.BlockSpec((ROW_TILE, d), lambda i, *_: (i, 0)),
                  pl.BlockSpec((1, d), lambda i, *_: (0, 0))],
        out_specs=[pl.BlockSpec((ROW_TILE, d), lambda i, *_: (jnp.minimum(i, npt - 1), 0)),
                   pl.BlockSpec((ROW_TILE, d), lambda i, *_: (jnp.maximum(i - npt, 0), 0))],
        scratch_shapes=[pltpu.VMEM((2, TOP_K, ROW_TILE, d), F32), pltpu.SemaphoreType.DMA((2,))],
    )
    return pl.pallas_call(
        functools.partial(_combine_kernel, n_prompt_tiles=npt),
        grid_spec=grid_spec,
        out_shape=[jax.ShapeDtypeStruct((n_prompt, d), F32), jax.ShapeDtypeStruct((n_sample, d), F32)],
        compiler_params=_params(("arbitrary",)),
        name="moe_combine",
    )(dest_flat, y_rows, probs, h1, nw)


def _rope_tables(pos0, t, dk):
    inv = 1.0 / (ROPE_BASE ** jnp.linspace(0.0, 1.0, dk // 2, dtype=F32))
    ang = (pos0 + jnp.arange(t, dtype=F32))[:, None] * inv[None, :]
    cos = jnp.repeat(jnp.cos(ang), 2, axis=-1)
    sin = jnp.repeat(jnp.sin(ang), 2, axis=-1)
    sign = jnp.where(jnp.arange(dk) % 2 == 0, -1.0, 1.0).astype(F32)
    return cos, sin * sign


def kernel(x_prompt, x_sample, state_gla, state_ret, norm_mix, w_in, gla_w_gk, gla_b_gk, gla_norm_w, w_out,
           norm_ffn, router_w, router_b, w_gate, b_gate, w_up, b_up, w_down, b_down, norm_final):
    bp, tp, d = x_prompt.shape
    bs, ts, _ = x_sample.shape
    assert ts == 1 and w_in.shape[0] == 1
    dk, dv = state_gla.shape[-2], state_gla.shape[-1]
    qk, vv = HEADS * dk, HEADS * dv
    n_p, n_s = bp * tp, bs * ts
    n = n_p + n_s
    ne = router_w.shape[-1]
    assert n_p % ROW_TILE == 0 and n_s % ROW_TILE == 0

    w = w_in[0]
    c_rank = 2 * qk + 2 * vv
    w_main = jnp.concatenate([w[:, :c_rank], w[:, c_rank + GLA_RANK:]], axis=1).astype(BF16)
    w_rank = jnp.pad(w[:, c_rank:c_rank + GLA_RANK], ((0, 0), (0, LANE - GLA_RANK))).astype(BF16)
    w_gk = jnp.pad(gla_w_gk[0], ((0, LANE - GLA_RANK), (0, 0))).astype(BF16)
    b_gk = gla_b_gk[0][None, :]
    cols = (0, qk, 2 * qk, 2 * qk + vv, c_rank, c_rank + qk, c_rank + 2 * qk, c_rank + 2 * qk + vv)
    nmix = norm_mix[0][None, :]
    wo = w_out[0].astype(BF16)
    wo_g, wo_r = wo[:vv], wo[vv:]
    rwt = router_w[0].T.astype(BF16)
    rb = router_b[0][:, None]
    nffn = norm_ffn[0][None, :]
    gnw = gla_norm_w[0][None, :]

    xp = x_prompt.reshape(n_p, d)
    proj_p, lg_p = _in_proj(xp, nmix, w_main, w_rank, w_gk, b_gk, tm=min(512, n_p), tn=1024)
    cos_p, sin_p = _rope_tables(0.0, tp, dk)
    og_p, st_gla_p = _gla_prompt(proj_p, lg_p, gnw, bp, tp, dk, dv, cols[0], cols[1], cols[2], cols[3])
    or_p, s_ret_p = _ret_prompt(proj_p, cos_p, sin_p, bp, tp, dk, dv, cols[4], cols[5], cols[6], cols[7])
    s_gla_p = jnp.swapaxes(st_gla_p, -1, -2)

    xs = x_sample.reshape(n_s, d)
    proj_s, lg_s = _in_proj(xs, nmix, w_main, w_rank, w_gk, b_gk, tm=n_s, tn=1024)
    cos_s, sin_s = _rope_tables(float(PAST_LEN), 1, dk)
    og_s, or_s, s_gla_s, s_ret_s = _step(proj_s, lg_s, cos_s, sin_s, gnw, state_gla[0], state_ret[0], dk, dv, cols)

    tm_p = min(256, n_p)
    h1, lt_p = _out_proj(og_p, or_p, xp, wo_g, wo_r, nffn, rwt, rb, tm_p, n, 0)
    h1, lt_s = _out_proj(og_s, or_s, xs, wo_g, wo_r, nffn, rwt, rb, n_s, n, n_p // n_s, h1_prev=h1)
    logits_t = jnp.concatenate([lt_p, lt_s], axis=1)

    n_slots = (n * TOP_K) // SLOT_ROWS + ne
    n_tiles_max = (n * TOP_K) // MM_ROWS + ne
    dest, probs_t, item_meta, tile_meta = _route(logits_t, n_tiles_max)
    pad_lo = item_meta[4, :ne]
    pad_hi = item_meta[5, :ne]
    dest_flat = dest.reshape(-1)
    experts = item_meta[0, :n_slots]
    nsub = item_meta[1, :n_slots]
    counts = item_meta[2:4, 0]
    tile_ids = tile_meta[0, :n_tiles_max]

    xs_rows = _gather(dest_flat, tile_ids, counts, pad_lo, pad_hi, h1, nffn, n_slots, n_tiles_max)
    hid = _moe_up(experts, nsub, counts, xs_rows.reshape(n_slots, SLOT_ROWS, d), w_gate[0], w_up[0],
                  b_gate[0][:, None, :], b_up[0][:, None, :], tn=256)
    y_rows = _moe_down(experts, nsub, counts, hid, w_down[0], b_down[0][:, None, :], tn=512)

    y_p, y_s = _combine(dest_flat, y_rows.reshape(n_slots * SLOT_ROWS, d), probs_t.T, h1,
                        norm_final[None, :], n_p, n_s)
    return (y_p.reshape(bp, tp, d), y_s.reshape(bs, ts, d),
            s_gla_p[None], s_ret_p[None], s_gla_s[None], s_ret_s[None])
```

```python
import functools
import math

import numpy as np
import jax
import jax.numpy as jnp
from jax import lax
from jax.experimental import pallas as pl
from jax.experimental.pallas import tpu as pltpu

F32 = jnp.float32
BF16 = jnp.bfloat16
I32 = jnp.int32

HEADS = 4
GLA_RANK = 16
GLA_GATE_NORM = 16.0
ROPE_BASE = 10000.0
TOP_K = 4
SWIGLU_LIMIT = 7.0
SWIGLU_ALPHA = 1.702
EPS = 1e-6
PAST_LEN = 16384

LANE = 128
ROW_TILE = 128
SLOT_TILES = 10
SLOT_ROWS = SLOT_TILES * ROW_TILE
MM_ROWS = 2 * ROW_TILE
SLOT_CHUNKS = SLOT_ROWS // MM_ROWS
GLA_CHUNK = 64
GLA_SUB = 16
RET_CHUNK = 256
VMEM_LIMIT = 60 * 1024 * 1024

_NT = (((1,), (1,)), ((), ()))
_TN = (((0,), (0,)), ((), ()))


def _params(sem, vmem=VMEM_LIMIT):
    return pltpu.CompilerParams(dimension_semantics=sem, vmem_limit_bytes=vmem)


def _sigmoid(x):
    return 1.0 / (1.0 + jnp.exp(-x))


def _rms_scale(x):
    return lax.rsqrt(jnp.mean(x * x, axis=-1, keepdims=True) + EPS)


def _norm_gate_kernel(x_ref, nw_ref, wr_ref, wgk_ref, bgk_ref, *rest):
    h_ref, lg_ref = rest[-2], rest[-1]
    x = x_ref[...]
    hb = (x * _rms_scale(x) * nw_ref[...]).astype(BF16)
    h_ref[...] = hb
    gr = jnp.dot(hb, wr_ref[...].astype(BF16), preferred_element_type=F32)
    z = jnp.dot(gr.astype(BF16), wgk_ref[...], preferred_element_type=F32) + bgk_ref[...]
    log_sig = jnp.minimum(z, 0.0) - jnp.log1p(jnp.exp(-jnp.abs(z)))
    lg_ref[...] = log_sig * (1.0 / GLA_GATE_NORM)


def _norm_gate(x, nw, w_in, rank_col, wgk, bgk, tm, n_total, row_block0, prev=None):
    m, d = x.shape
    assert m % tm == 0 and rank_col % LANE == 0
    nlg = wgk.shape[1]
    in_specs = [
        pl.BlockSpec((tm, d), lambda i: (i, 0)),
        pl.BlockSpec((1, d), lambda i: (0, 0)),
        pl.BlockSpec((d, LANE), lambda i: (0, rank_col // LANE)),
        pl.BlockSpec((LANE, nlg), lambda i: (0, 0)),
        pl.BlockSpec((1, nlg), lambda i: (0, 0)),
    ]
    args = [x, nw, w_in, wgk, bgk]
    aliases = {}
    if prev is not None:
        in_specs += [pl.BlockSpec(memory_space=pl.ANY)] * 2
        args += list(prev)
        aliases = {len(args) - 2: 0, len(args) - 1: 1}
    return pl.pallas_call(
        _norm_gate_kernel,
        grid=(m // tm,),
        in_specs=in_specs,
        out_specs=[pl.BlockSpec((tm, d), lambda i: (row_block0 + i, 0)),
                   pl.BlockSpec((tm, nlg), lambda i: (row_block0 + i, 0))],
        out_shape=[jax.ShapeDtypeStruct((n_total, d), BF16), jax.ShapeDtypeStruct((n_total, nlg), F32)],
        input_output_aliases=aliases,
        compiler_params=_params(("parallel",)),
        name="norm_gate",
    )(*args)


def _in_proj_kernel(h_ref, wm_ref, wx_ref, proj_ref, w_scr, *, first_shifted):
    j = pl.program_id(0)
    tn = wm_ref.shape[1]

    @pl.when(jnp.logical_and(pl.program_id(1) == 0, j < first_shifted))
    def _():
        w_scr[...] = wm_ref[...].astype(BF16)

    @pl.when(jnp.logical_and(pl.program_id(1) == 0, j >= first_shifted))
    def _():
        w = jnp.concatenate([wm_ref[...], wx_ref[...]], axis=1)
        w_scr[...] = w[:, GLA_RANK:GLA_RANK + tn].astype(BF16)

    proj_ref[...] = jnp.dot(h_ref[...], w_scr[...], preferred_element_type=F32)


def _in_proj(h, w_in, rank_col, tm, tn):
    m, d = h.shape
    n = w_in.shape[1] - GLA_RANK
    assert m % tm == 0 and n % tn == 0 and rank_col % tn == 0 and tn % LANE == 0
    per = tn // LANE
    return pl.pallas_call(
        functools.partial(_in_proj_kernel, first_shifted=rank_col // tn),
        grid=(n // tn, m // tm),
        in_specs=[
            pl.BlockSpec((tm, d), lambda j, i: (i, 0)),
            pl.BlockSpec((d, tn), lambda j, i: (0, j)),
            pl.BlockSpec((d, LANE), lambda j, i: (0, (j + 1) * per)),
        ],
        out_specs=pl.BlockSpec((tm, tn), lambda j, i: (i, j)),
        out_shape=jax.ShapeDtypeStruct((m, n), F32),
        scratch_shapes=[pltpu.VMEM((d, tn), BF16)],
        compiler_params=_params(("arbitrary", "arbitrary")),
        name="in_proj",
    )(h, w_in, w_in)


def _rope(x, cos, sin_signed, even):
    n = x.shape[-1]
    nxt = pltpu.roll(x, n - 1, axis=x.ndim - 1)
    prv = pltpu.roll(x, 1, axis=x.ndim - 1)
    return x * cos + jnp.where(even, nxt, prv) * sin_signed


def _ret_prompt_kernel(q_ref, k_ref, v_ref, g_ref, cos_ref, sin_ref, o_ref, s_ref):
    c = q_ref.shape[0]
    dk = q_ref.shape[1] // HEADS
    dv = v_ref.shape[1] // HEADS

    @pl.when(pl.program_id(1) == 0)
    def _():
        s_ref[...] = jnp.zeros_like(s_ref)

    ti = lax.broadcasted_iota(I32, (c, c), 0)
    si = lax.broadcasted_iota(I32, (c, c), 1)
    dlt = (ti - si).astype(F32)
    causal = ti >= si
    tcol = lax.broadcasted_iota(I32, (c, 1), 0).astype(F32)
    even = (lax.broadcasted_iota(I32, (c, dk), 1) % 2) == 0
    cos = cos_ref[...]
    sin = sin_ref[...]
    for h in range(HEADS):
        lgam = math.log(1.0 - 2.0 ** (-5.0 - h))
        q = _rope(q_ref[:, h * dk:(h + 1) * dk], cos, sin, even)
        k = _rope(k_ref[:, h * dk:(h + 1) * dk], cos, sin, even) * (dk ** -0.5)
        qb = q.astype(BF16)
        vb = v_ref[:, h * dv:(h + 1) * dv].astype(BF16)
        decay = jnp.where(causal, jnp.exp(dlt * lgam), 0.0)
        a = lax.dot_general(qb, k.astype(BF16), _NT, preferred_element_type=F32) * decay
        s = s_ref[0, h]
        o = jnp.dot(a.astype(BF16), vb, preferred_element_type=F32)
        o = o + jnp.dot(qb, s.astype(BF16), preferred_element_type=F32) * jnp.exp((tcol + 1.0) * lgam)
        kd = (k * jnp.exp((c - 1.0 - tcol) * lgam)).astype(BF16)
        s_ref[0, h] = math.exp(c * lgam) * s + lax.dot_general(kd, vb, _TN, preferred_element_type=F32)
        g = g_ref[:, h * dv:(h + 1) * dv]
        o_ref[:, h * dv:(h + 1) * dv] = (o * _rms_scale(o) * (g * _sigmoid(g))).astype(BF16)


def _ret_prompt(proj, cos, sin, batch, seq, dk, dv, col_q, col_k, col_v, col_g):
    c = math.gcd(seq, RET_CHUNK)
    nt = seq // c
    qw, vw = HEADS * dk, HEADS * dv
    return pl.pallas_call(
        _ret_prompt_kernel,
        grid=(batch, nt),
        in_specs=[
            pl.BlockSpec((c, qw), lambda b, t: (b * nt + t, col_q // qw)),
            pl.BlockSpec((c, qw), lambda b, t: (b * nt + t, col_k // qw)),
            pl.BlockSpec((c, vw), lambda b, t: (b * nt + t, col_v // vw)),
            pl.BlockSpec((c, vw), lambda b, t: (b * nt + t, col_g // vw)),
            pl.BlockSpec((c, dk), lambda b, t: (t, 0)),
            pl.BlockSpec((c, dk), lambda b, t: (t, 0)),
        ],
        out_specs=[
            pl.BlockSpec((c, vw), lambda b, t: (b * nt + t, 0)),
            pl.BlockSpec((1, HEADS, dk, dv), lambda b, t: (b, 0, 0, 0)),
        ],
        out_shape=[jax.ShapeDtypeStruct((batch * seq, vw), BF16),
                   jax.ShapeDtypeStruct((batch, HEADS, dk, dv), F32)],
        compiler_params=_params(("parallel", "arbitrary")),
        name="ret_prompt",
    )(proj, proj, proj, proj, cos, sin)


def _gla_chunk(q_ref, k_ref, v_ref, g_ref, lg_ref, nw_ref, o_ref, st_ref, b_scr, a_scr):
    c = q_ref.shape[0]
    dk = q_ref.shape[1] // HEADS
    dv = v_ref.shape[1] // HEADS
    sub = GLA_SUB
    nsub = c // sub

    @pl.when(pl.program_id(1) == 0)
    def _():
        st_ref[...] = jnp.zeros_like(st_ref)

    ri = lax.broadcasted_iota(I32, (c, c), 0)
    ci = lax.broadcasted_iota(I32, (c, c), 1)
    lower = (ri >= ci).astype(BF16)
    cis = lax.broadcasted_iota(I32, (sub, c), 1)

    for h in range(HEADS):
        lg = lg_ref[:, h * dk:(h + 1) * dk]
        l1 = lg.astype(BF16)
        r1 = lg - l1.astype(F32)
        l2 = r1.astype(BF16)
        l3 = (r1 - l2.astype(F32)).astype(BF16)
        b_scr[h] = (jnp.dot(lower, l1, preferred_element_type=F32)
                    + jnp.dot(lower, l2, preferred_element_type=F32)
                    + jnp.dot(lower, l3, preferred_element_type=F32))

    def sub_block(i, carry):
        r0 = pl.multiple_of(i * sub, sub)
        rprev = jnp.maximum(r0 - 1, 0)
        for h in range(HEADS):
            hs = slice(h * dk, (h + 1) * dk)
            qi = q_ref[pl.ds(r0, sub), hs] * (dk ** -0.5)
            ki = k_ref[pl.ds(r0, sub), hs]
            bi = b_scr[h, pl.ds(r0, sub), :]
            ad = jnp.zeros((sub, c), F32)
            for s in range(sub):
                e = jnp.exp(jnp.minimum(bi - bi[s:s + 1, :], 0.0))
                col = jnp.sum(qi * e * ki[s:s + 1, :], axis=1, keepdims=True)
                ad = jnp.where(cis == r0 + s, col, ad)
            bref = b_scr[h, pl.ds(rprev, 1), :]
            qq = (qi * jnp.exp(jnp.minimum(bi - bref, 0.0))).astype(BF16)
            kk = (k_ref[:, hs] * jnp.exp(jnp.minimum(bref - b_scr[h], 0.0))).astype(BF16)
            ao = lax.dot_general(qq, kk, _NT, preferred_element_type=F32)
            a_scr[h, pl.ds(r0, sub), :] = jnp.where(cis < r0, ao, ad)
        return carry

    lax.fori_loop(0, nsub, sub_block, 0)

    for h in range(HEADS):
        hs = slice(h * dk, (h + 1) * dk)
        vs = slice(h * dv, (h + 1) * dv)
        b = b_scr[h]
        b_last = b[c - 1:c, :]
        vb = v_ref[:, vs].astype(BF16)
        qe = (q_ref[:, hs] * (dk ** -0.5) * jnp.exp(b)).astype(BF16)
        kd = (k_ref[:, hs] * jnp.exp(b_last - b)).astype(BF16)
        st = st_ref[0, h]
        a = jnp.where(ri >= ci, a_scr[h], 0.0)
        o = jnp.dot(a.astype(BF16), vb, preferred_element_type=F32)
        o = o + lax.dot_general(qe, st.astype(BF16), _NT, preferred_element_type=F32)
        st_ref[0, h] = st * jnp.exp(b_last) + lax.dot_general(vb, kd, _TN, preferred_element_type=F32)
        g = g_ref[:, vs]
        o_ref[:, vs] = (o * _rms_scale(o) * nw_ref[...] * (g * _sigmoid(g))).astype(BF16)


def _to_col(row, eye):
    return jnp.sum(jnp.where(eye, row, 0.0), axis=1, keepdims=True)


def _step_rows(b, seq, proj_ref, lg_ref, cos_ref, sin_ref, nw_ref, sg_ref, sr_ref,
               og_ref, or_ref, ng_ref, nr_ref, cols):
    dk, dv = sg_ref.shape[-2], sg_ref.shape[-1]
    col_gq, col_gk, col_gv, col_gg, col_rq, col_rk, col_rv, col_rg = cols
    eye = lax.broadcasted_iota(I32, (dk, dk), 0) == lax.broadcasted_iota(I32, (dk, dk), 1)
    even = (lax.broadcasted_iota(I32, (1, dk), 1) % 2) == 0
    row = pl.ds(b, 1)

    def finish(o, g, w):
        o = o * _rms_scale(o)
        if w is not None:
            o = o * w
        return o * (g * _sigmoid(g))

    for h in range(HEADS):
        q = proj_ref[row, col_gq + h * dk:col_gq + (h + 1) * dk] * (dk ** -0.5)
        k = proj_ref[row, col_gk + h * dk:col_gk + (h + 1) * dk]
        v = proj_ref[row, col_gv + h * dv:col_gv + (h + 1) * dv]
        g = proj_ref[row, col_gg + h * dv:col_gg + (h + 1) * dv]
        a = jnp.exp(lg_ref[row, h * dk:(h + 1) * dk])
        s_new = _to_col(a, eye) * sg_ref[seq, h] + _to_col(k, eye) * v
        ng_ref[seq, h] = s_new
        o = jnp.sum(_to_col(q, eye) * s_new, axis=0, keepdims=True)
        og_ref[row, h * dv:(h + 1) * dv] = finish(o, g, nw_ref[...])

        gamma = 1.0 - 2.0 ** (-5.0 - h)
        q = _rope(proj_ref[row, col_rq + h * dk:col_rq + (h + 1) * dk], cos_ref[...], sin_ref[...], even)
        k = _rope(proj_ref[row, col_rk + h * dk:col_rk + (h + 1) * dk], cos_ref[...], sin_ref[...], even)
        k = k * (dk ** -0.5)
        v = proj_ref[row, col_rv + h * dv:col_rv + (h + 1) * dv]
        g = proj_ref[row, col_rg + h * dv:col_rg + (h + 1) * dv]
        s_new = gamma * sr_ref[seq, h] + _to_col(k, eye) * v
        nr_ref[seq, h] = s_new
        o = jnp.sum(_to_col(q, eye) * s_new, axis=0, keepdims=True)
        or_ref[row, h * dv:(h + 1) * dv] = finish(o, g, None)


def _gla_step_kernel(q_ref, k_ref, v_ref, g_ref, lg_ref, nw_ref, sproj_ref, slg_ref, cos_ref, sin_ref,
                     sg_ref, sr_ref, o_ref, st_ref, og_ref, or_ref, ng_ref, nr_ref, b_scr, a_scr,
                     *, nt, cols):
    _gla_chunk(q_ref, k_ref, v_ref, g_ref, lg_ref, nw_ref, o_ref, st_ref, b_scr, a_scr)
    per_step = sg_ref.shape[0]
    first = (pl.program_id(0) * nt + pl.program_id(1)) * per_step
    for s in range(per_step):
        _step_rows(first + s, s, sproj_ref, slg_ref, cos_ref, sin_ref, nw_ref, sg_ref, sr_ref,
                   og_ref, or_ref, ng_ref, nr_ref, cols)


def _gla_step(proj, lg, nw, cos_s, sin_s, s_gla, s_ret, batch, seq, sample_row_block, cols):
    nb, _, dk, dv = s_gla.shape
    c = math.gcd(seq, GLA_CHUNK)
    nt = seq // c
    assert nb % (batch * nt) == 0
    per_step = nb // (batch * nt)
    qw, vw = HEADS * dk, HEADS * dv
    ncol = proj.shape[1]
    col_q, col_k, col_v, col_g = cols[:4]
    blk = lambda width, col: pl.BlockSpec((c, width), lambda b, t: (b * nt + t, col // width))
    full = lambda shape: pl.BlockSpec(shape, lambda b, t: (0,) * len(shape))
    rows = lambda width: pl.BlockSpec((nb, width), lambda b, t: (sample_row_block, 0))
    state = pl.BlockSpec((per_step, HEADS, dk, dv), lambda b, t: (b * nt + t, 0, 0, 0))
    return pl.pallas_call(
        functools.partial(_gla_step_kernel, nt=nt, cols=cols),
        grid=(batch, nt),
        in_specs=[blk(qw, col_q), blk(qw, col_k), blk(vw, col_v), blk(vw, col_g), blk(qw, 0), full((1, dv)),
                  rows(ncol), rows(qw), full((1, dk)), full((1, dk)), state, state],
        out_specs=[
            pl.BlockSpec((c, vw), lambda b, t: (b * nt + t, 0)),
            pl.BlockSpec((1, HEADS, dv, dk), lambda b, t: (b, 0, 0, 0)),
            full((nb, vw)), full((nb, vw)), state, state,
        ],
        out_shape=[jax.ShapeDtypeStruct((batch * seq, vw), BF16),
                   jax.ShapeDtypeStruct((batch, HEADS, dv, dk), F32),
                   jax.ShapeDtypeStruct((nb, vw), F32), jax.ShapeDtypeStruct((nb, vw), F32),
                   jax.ShapeDtypeStruct(s_gla.shape, F32), jax.ShapeDtypeStruct(s_ret.shape, F32)],
        scratch_shapes=[pltpu.VMEM((HEADS, c, dk), F32), pltpu.VMEM((HEADS, c, c), F32)],
        compiler_params=_params(("arbitrary", "arbitrary")),
        name="gla_step",
    )(proj, proj, proj, proj, lg, nw, proj, lg, cos_s, sin_s, s_gla, s_ret)


def _out_proj_kernel(og_ref, or_ref, x_ref, wg_ref, wr_ref, nw_ref, rwt_ref, rb_ref, *rest):
    h1_ref, lt_ref = rest[-2], rest[-1]
    acc = jnp.dot(og_ref[...].astype(BF16), wg_ref[...], preferred_element_type=F32)
    acc = acc + jnp.dot(or_ref[...].astype(BF16), wr_ref[...], preferred_element_type=F32)
    h1 = x_ref[...] + acc
    h1_ref[...] = h1
    flat = (h1 * _rms_scale(h1) * nw_ref[...]).astype(BF16)
    lt_ref[...] = lax.dot_general(rwt_ref[...], flat, _NT, preferred_element_type=F32) + rb_ref[...]


def _out_proj(og, orr, x, wo, nw, rwt, rb, tm, n_total, row_block0, h1_prev=None):
    m, d = x.shape
    half = og.shape[1]
    ne = rwt.shape[0]
    const = lambda shape: pl.BlockSpec(shape, lambda i: (0,) * len(shape), pipeline_mode=pl.Buffered(1))
    in_specs = [
        pl.BlockSpec((tm, half), lambda i: (i, 0)),
        pl.BlockSpec((tm, half), lambda i: (i, 0)),
        pl.BlockSpec((tm, d), lambda i: (i, 0)),
        const((half, d)), pl.BlockSpec((half, d), lambda i: (1, 0), pipeline_mode=pl.Buffered(1)),
        const((1, d)), const((ne, d)), const((ne, 1)),
    ]
    args = [og, orr, x, wo, wo, nw, rwt, rb]
    aliases = {}
    if h1_prev is not None:
        in_specs.append(pl.BlockSpec(memory_space=pl.ANY))
        args.append(h1_prev)
        aliases = {len(args) - 1: 0}
    return pl.pallas_call(
        _out_proj_kernel,
        grid=(m // tm,),
        in_specs=in_specs,
        out_specs=[
            pl.BlockSpec((tm, d), lambda i: (row_block0 + i, 0)),
            pl.BlockSpec((ne, tm), lambda i: (0, i)),
        ],
        out_shape=[jax.ShapeDtypeStruct((n_total, d), F32), jax.ShapeDtypeStruct((ne, m), F32)],
        input_output_aliases=aliases,
        compiler_params=_params(("parallel",)),
        name="out_proj",
    )(*args)


def _route_kernel(lt_ref, dest_ref, prob_ref, item_ref, tile_ref, m_scr, pos_scr):
    ne, n = lt_ref.shape
    nblk = n // LANE
    logits = lt_ref[...]
    eio = lax.broadcasted_iota(I32, (ne, n), 0).astype(F32)
    vals, idxs = [], []
    for _ in range(TOP_K):
        m = jnp.max(logits, axis=0, keepdims=True)
        ik = jnp.min(jnp.where(logits == m, eio, float(ne)), axis=0, keepdims=True)
        vals.append(m)
        idxs.append(ik)
        logits = jnp.where(eio == ik, -jnp.inf, logits)
    ex = [jnp.exp(v - vals[0]) for v in vals]
    den = ex[0] + ex[1] + ex[2] + ex[3]
    for k in range(TOP_K):
        prob_ref[k:k + 1, :] = ex[k] / den
    sel = jnp.zeros((ne, n), F32)
    for ik in idxs:
        sel = jnp.where(eio == ik, 1.0, sel)
    m_scr[...] = sel

    upper = (lax.broadcasted_iota(I32, (LANE, LANE), 0) <= lax.broadcasted_iota(I32, (LANE, LANE), 1)).astype(BF16)

    def prefix(cb, carry):
        c0 = pl.multiple_of(cb * LANE, LANE)
        mc = m_scr[:, pl.ds(c0, LANE)]
        incl = jnp.dot(mc.astype(BF16), upper, preferred_element_type=F32)
        pos_scr[:, pl.ds(c0, LANE)] = incl - mc + carry
        return carry + jnp.sum(mc, axis=1, keepdims=True)

    cnt = lax.fori_loop(0, nblk, prefix, jnp.zeros((ne, 1), F32))

    def ceil_div(x, step, most):
        r = jnp.zeros_like(x)
        for mlt in range(most):
            r = r + jnp.where(x > float(mlt * step), 1.0, 0.0)
        return r

    cnt_l = jnp.broadcast_to(cnt, (ne, LANE))
    n_slot = ceil_div(cnt_l, SLOT_ROWS, -(-n // SLOT_ROWS))
    n_tile = ceil_div(cnt_l, MM_ROWS, -(-n // MM_ROWS))
    strict = (lax.broadcasted_iota(I32, (ne, ne), 0) > lax.broadcasted_iota(I32, (ne, ne), 1)).astype(BF16)
    slot0 = jnp.dot(strict, n_slot.astype(BF16), preferred_element_type=F32)
    tile0 = jnp.dot(strict, n_tile.astype(BF16), preferred_element_type=F32)

    destf = slot0[:, :1] * float(SLOT_ROWS) + pos_scr[...]
    for k in range(TOP_K):
        dk_ = jnp.sum(jnp.where(eio == idxs[k], destf, 0.0), axis=0, keepdims=True)
        dest_ref[k:k + 1, :] = dk_.astype(I32)

    def lookup(table, onehot):
        return jnp.sum(jnp.where(onehot, table, 0.0), axis=0, keepdims=True)

    e_col = lax.broadcasted_iota(I32, (ne, LANE), 0).astype(F32)
    it = lax.broadcasted_iota(I32, (ne, LANE), 1).astype(F32)
    n_items = jnp.sum(n_slot, axis=0, keepdims=True)
    e_it = jnp.minimum(jnp.sum(jnp.where(slot0 + n_slot <= it, 1.0, 0.0), axis=0, keepdims=True), ne - 1.0)
    oh = e_col == e_it
    rows = lookup(cnt_l, oh) - (it[:1] - lookup(slot0, oh)) * float(SLOT_ROWS)
    rows = jnp.clip(rows, 0.0, float(SLOT_ROWS))
    rows = jnp.where(it[:1] < n_items, rows, 0.0)
    item_ref[...] = jnp.zeros_like(item_ref)
    item_ref[0:1, :] = e_it.astype(I32)
    item_ref[1:2, :] = ceil_div(rows, ROW_TILE, SLOT_TILES).astype(I32)
    item_ref[2:3, :] = n_items.astype(I32)
    item_ref[3:4, :] = jnp.sum(n_tile, axis=0, keepdims=True).astype(I32)
    diag = e_col == it
    item_ref[4:5, :] = lookup(slot0 * float(SLOT_ROWS) + cnt_l, diag).astype(I32)
    item_ref[5:6, :] = lookup(slot0 * float(SLOT_ROWS) + n_tile * float(MM_ROWS), diag).astype(I32)

    ntl = tile_ref.shape[1]
    tile_ref[...] = jnp.zeros_like(tile_ref)
    for cb in range(ntl // LANE):
        gi = it + float(cb * LANE)
        e_g = jnp.minimum(jnp.sum(jnp.where(tile0 + n_tile <= gi, 1.0, 0.0), axis=0, keepdims=True), ne - 1.0)
        ohg = e_col == e_g
        tid = lookup(slot0, ohg) * float(SLOT_CHUNKS) + (gi[:1] - lookup(tile0, ohg))
        tile_ref[0:1, cb * LANE:(cb + 1) * LANE] = tid.astype(I32)


def _route(logits_t, n_tiles_max):
    ne, n = logits_t.shape
    ntl = -(-n_tiles_max // LANE) * LANE
    return pl.pallas_call(
        _route_kernel,
        out_shape=[jax.ShapeDtypeStruct((TOP_K, n), I32), jax.ShapeDtypeStruct((TOP_K, n), F32),
                   jax.ShapeDtypeStruct((8, LANE), I32), jax.ShapeDtypeStruct((8, ntl), I32)],
        scratch_shapes=[pltpu.VMEM((ne, n), F32), pltpu.VMEM((ne, n), F32)],
        compiler_params=pltpu.CompilerParams(vmem_limit_bytes=VMEM_LIMIT),
        name="route",
    )(logits_t)


def _gather_kernel(dest_sm, tile_sm, cnt_sm, pad_lo_sm, pad_hi_sm, h1_hbm, nw_ref, xs_ref, inv_sm, buf, sem):
    g = pl.program_id(0)
    n_tiles = cnt_sm[1]
    n_tok = dest_sm.shape[0] // TOP_K
    group = 16

    def issue(gi, slot):
        base = tile_sm[gi] * MM_ROWS

        def body(r, carry):
            src = inv_sm[base + r]
            pltpu.make_async_copy(h1_hbm.at[pl.ds(src, 1)], buf.at[slot, pl.ds(r, 1)], sem.at[slot]).start()
            return carry

        lax.fori_loop(0, MM_ROWS, body, 0, unroll=8)

    @pl.when(g == 0)
    def _():
        spread = (1 << (n_tok.bit_length() - 1)) - 1

        def pad_expert(e, carry):
            def clear(i, c2):
                inv_sm[i] = i & spread
                return c2

            lax.fori_loop(pad_lo_sm[e], pad_hi_sm[e], clear, 0)
            return carry

        lax.fori_loop(0, pad_lo_sm.shape[0], pad_expert, 0)
        for k in range(TOP_K):
            def scatter(t, carry):
                inv_sm[dest_sm[k * n_tok + t]] = t
                return carry

            lax.fori_loop(0, n_tok, scatter, 0, unroll=8)
        issue(0, 0)

    @pl.when(g + 1 < n_tiles)
    def _():
        issue(g + 1, (g + 1) % 2)

    @pl.when(g < n_tiles)
    def _():
        slot = g % 2
        pltpu.make_async_copy(h1_hbm.at[pl.ds(0, MM_ROWS)], buf.at[slot], sem.at[slot]).wait()

        def norm_rows(r, carry):
            r0 = pl.multiple_of(r * group, group)
            x = buf[slot, pl.ds(r0, group), :]
            xs_ref[pl.ds(r0, group), :] = (x * _rms_scale(x) * nw_ref[...]).astype(BF16)
            return carry

        lax.fori_loop(0, MM_ROWS // group, norm_rows, 0, unroll=2)


def _gather(dest_flat, tile_ids, counts, pad_lo, pad_hi, h1, nw, n_slots, n_tiles_max):
    d = h1.shape[1]

    def out_map(g, dest_sm, tile_sm, cnt_sm, *_):
        return (tile_sm[jnp.minimum(g, cnt_sm[1] - 1)], 0)

    grid_spec = pltpu.PrefetchScalarGridSpec(
        num_scalar_prefetch=5,
        grid=(n_tiles_max,),
        in_specs=[pl.BlockSpec(memory_space=pl.ANY),
                  pl.BlockSpec((1, d), lambda g, *_: (0, 0))],
        out_specs=pl.BlockSpec((MM_ROWS, d), out_map),
        scratch_shapes=[pltpu.SMEM((n_slots * SLOT_ROWS,), I32),
                        pltpu.VMEM((2, MM_ROWS, d), F32),
                        pltpu.SemaphoreType.DMA((2,))],
    )
    return pl.pallas_call(
        _gather_kernel,
        grid_spec=grid_spec,
        out_shape=jax.ShapeDtypeStruct((n_slots * SLOT_ROWS, d), BF16),
        compiler_params=_params(("arbitrary",)),
        name="moe_gather",
    )(dest_flat, tile_ids, counts, pad_lo, pad_hi, h1, nw)


def _for_row_chunks(ns, matmul, finish, clear, acc_scr):
    acc_a, acc_b = acc_scr
    odd = jnp.logical_and(ns > 1, (ns & 1) == 1)
    n_full = jnp.maximum(lax.shift_right_logical(ns, 1), 1)
    n_pair = lax.shift_right_logical(n_full - 1, 1)
    row0 = lambda c: pl.multiple_of(c * MM_ROWS, MM_ROWS)
    acc_a[...] = matmul(0, MM_ROWS)

    def pair(p, carry):
        c = 2 * p
        acc_b[...] = matmul(row0(c + 1), MM_ROWS)
        finish(row0(c), MM_ROWS, acc_a[...])
        acc_a[...] = matmul(row0(c + 2), MM_ROWS)
        finish(row0(c + 1), MM_ROWS, acc_b[...])
        return carry

    lax.fori_loop(0, n_pair, pair, 0)
    done = 2 * n_pair
    extra = (n_full - 1) > done
    t0 = pl.multiple_of((ns - 1) * ROW_TILE, ROW_TILE)

    @pl.when(extra)
    def _():
        acc_b[...] = matmul(row0(done + 1), MM_ROWS)
        finish(row0(done), MM_ROWS, acc_a[...])

    def drain(acc, c):
        @pl.when(odd)
        def _():
            tail = matmul(t0, ROW_TILE)
            finish(row0(c), MM_ROWS, acc[...])
            finish(t0, ROW_TILE, tail)

        @pl.when(jnp.logical_not(odd))
        def _():
            finish(row0(c), MM_ROWS, acc[...])

    @pl.when(extra)
    def _():
        drain(acc_b, done + 1)

    @pl.when(jnp.logical_not(extra))
    def _():
        drain(acc_a, done)

    def clear_tile(s, carry):
        clear(pl.multiple_of(s * ROW_TILE, ROW_TILE))
        return carry

    lax.fori_loop(jnp.maximum(ns, MM_ROWS // ROW_TILE), SLOT_TILES, clear_tile, 0)


def _moe_up_kernel(exp_sm, nsub_sm, cnt_sm, x_ref, wg_ref, wu_ref, bg_ref, bu_ref, hid_ref, w_scr, *acc_scr):
    it = pl.program_id(0)
    tn = wg_ref.shape[2]

    @pl.when(it < cnt_sm[0])
    def _():
        w_scr[:, :tn] = wg_ref[0].astype(BF16)
        w_scr[:, tn:] = wu_ref[0].astype(BF16)

        def matmul(r0, rows):
            return jnp.dot(x_ref[0, pl.ds(r0, rows), :], w_scr[...], preferred_element_type=F32)

        def finish(r0, rows, gu):
            g = jnp.minimum(gu[:, :tn] + bg_ref[0], SWIGLU_LIMIT)
            u = jnp.clip(gu[:, tn:] + bu_ref[0], -SWIGLU_LIMIT, SWIGLU_LIMIT)
            hid = (u + 1.0) * g * _sigmoid(SWIGLU_ALPHA * g)
            hid_ref[0, pl.ds(r0, rows), :] = hid.astype(BF16)

        def clear(r0):
            hid_ref[0, pl.ds(r0, ROW_TILE), :] = jnp.zeros((ROW_TILE, tn), BF16)

        _for_row_chunks(nsub_sm[it], matmul, finish, clear, acc_scr)


def _moe_down_kernel(exp_sm, nsub_sm, cnt_sm, h_ref, w_ref, b_ref, y_ref, w_scr, *acc_scr):
    it = pl.program_id(0)
    tn = w_ref.shape[2]

    @pl.when(it < cnt_sm[0])
    def _():
        w_scr[...] = w_ref[0].astype(BF16)

        def matmul(r0, rows):
            return jnp.dot(h_ref[0, pl.ds(r0, rows), :], w_scr[...], preferred_element_type=F32)

        def finish(r0, rows, y):
            y_ref[0, pl.ds(r0, rows), :] = y + b_ref[0]

        def clear(r0):
            y_ref[0, pl.ds(r0, ROW_TILE), :] = jnp.zeros((ROW_TILE, tn), F32)

        _for_row_chunks(nsub_sm[it], matmul, finish, clear, acc_scr)


def _item_maps(nj):
    def eff(it, j, cnt_sm):
        last = cnt_sm[0] - 1
        return jnp.minimum(it, last), jnp.where(it <= last, j, nj - 1)

    def x_map(it, j, exp_sm, nsub_sm, cnt_sm):
        ie, _ = eff(it, j, cnt_sm)
        return (ie, 0, 0)

    def w_map(it, j, exp_sm, nsub_sm, cnt_sm):
        ie, je = eff(it, j, cnt_sm)
        return (exp_sm[ie], 0, je)

    def o_map(it, j, exp_sm, nsub_sm, cnt_sm):
        ie, je = eff(it, j, cnt_sm)
        return (ie, 0, je)

    return x_map, w_map, o_map


def _moe_up(experts, nsub, counts, xs, w_gate, w_up, b_gate, b_up, tn):
    n_slots, rows, d = xs.shape
    dff = w_gate.shape[2]
    nj = dff // tn
    x_map, w_map, o_map = _item_maps(nj)
    grid_spec = pltpu.PrefetchScalarGridSpec(
        num_scalar_prefetch=3,
        grid=(counts[0], nj),
        in_specs=[pl.BlockSpec((1, rows, d), x_map),
                  pl.BlockSpec((1, d, tn), w_map), pl.BlockSpec((1, d, tn), w_map),
                  pl.BlockSpec((1, 1, tn), w_map), pl.BlockSpec((1, 1, tn), w_map)],
        out_specs=pl.BlockSpec((1, rows, tn), o_map),
        scratch_shapes=[pltpu.VMEM((d, 2 * tn), BF16),
                        pltpu.VMEM((MM_ROWS, 2 * tn), F32), pltpu.VMEM((MM_ROWS, 2 * tn), F32)],
    )
    return pl.pallas_call(
        _moe_up_kernel,
        grid_spec=grid_spec,
        out_shape=jax.ShapeDtypeStruct((n_slots, rows, dff), BF16),
        compiler_params=_params(("arbitrary", "arbitrary")),
        name="moe_up",
    )(experts, nsub, counts, xs, w_gate, w_up, b_gate, b_up)


def _moe_down(experts, nsub, counts, hid, w_down, b_down, tn):
    n_slots, rows, dff = hid.shape
    d = w_down.shape[2]
    nj = d // tn
    x_map, w_map, o_map = _item_maps(nj)
    grid_spec = pltpu.PrefetchScalarGridSpec(
        num_scalar_prefetch=3,
        grid=(counts[0], nj),
        in_specs=[pl.BlockSpec((1, rows, dff), x_map),
                  pl.BlockSpec((1, dff, tn), w_map),
                  pl.BlockSpec((1, 1, tn), w_map)],
        out_specs=pl.BlockSpec((1, rows, tn), o_map),
        scratch_shapes=[pltpu.VMEM((dff, tn), BF16),
                        pltpu.VMEM((MM_ROWS, tn), F32), pltpu.VMEM((MM_ROWS, tn), F32)],
    )
    return pl.pallas_call(
        _moe_down_kernel,
        grid_spec=grid_spec,
        out_shape=jax.ShapeDtypeStruct((n_slots, rows, d), F32),
        compiler_params=_params(("arbitrary", "arbitrary")),
        name="moe_down",
    )(experts, nsub, counts, hid, w_down, b_down)


def _combine_kernel(dest_sm, y_hbm, p_ref, h1_ref, nw_ref, yp_ref, ys_ref, buf, sem, *, n_prompt_tiles):
    i = pl.program_id(0)
    n_steps = pl.num_programs(0)
    n_tok = dest_sm.shape[0] // TOP_K

    def issue(ti, slot):
        for k in range(TOP_K):
            def body(r, carry):
                src = dest_sm[k * n_tok + ti * ROW_TILE + r]
                pltpu.make_async_copy(y_hbm.at[pl.ds(src, 1)], buf.at[slot, k, pl.ds(r, 1)], sem.at[slot]).start()
                return carry

            lax.fori_loop(0, ROW_TILE, body, 0, unroll=8)

    @pl.when(i == 0)
    def _():
        issue(0, 0)

    @pl.when(i + 1 < n_steps)
    def _():
        issue(i + 1, (i + 1) % 2)

    slot = i % 2
    for k in range(TOP_K):
        pltpu.make_async_copy(y_hbm.at[pl.ds(0, ROW_TILE)], buf.at[slot, k], sem.at[slot]).wait()
    group = 16

    def combine_into(out_ref):
        def rows(r, carry):
            r0 = pl.multiple_of(r * group, group)
            acc = h1_ref[pl.ds(r0, group), :]
            for k in range(TOP_K):
                acc = acc + p_ref[pl.ds(r0, group), k:k + 1] * buf[slot, k, pl.ds(r0, group), :]
            out_ref[pl.ds(r0, group), :] = acc * _rms_scale(acc) * nw_ref[...]
            return carry

        lax.fori_loop(0, ROW_TILE // group, rows, 0, unroll=2)

    @pl.when(i < n_prompt_tiles)
    def _():
        combine_into(yp_ref)

    @pl.when(i >= n_prompt_tiles)
    def _():
        combine_into(ys_ref)


def _combine(dest_flat, y_rows, probs, h1, nw, n_prompt, n_sample):
    n, d = h1.shape
    npt = n_prompt // ROW_TILE
    grid_spec = pltpu.PrefetchScalarGridSpec(
        num_scalar_prefetch=1,
        grid=(n // ROW_TILE,),
        in_specs=[pl.BlockSpec(memory_space=pl.ANY),
                  pl.BlockSpec((ROW_TILE, TOP_K), lambda i, *_: (i, 0)),
                  pl.BlockSpec((ROW_TILE, d), lambda i, *_: (i, 0)),
                  pl.BlockSpec((1, d), lambda i, *_: (0, 0))],
        out_specs=[pl.BlockSpec((ROW_TILE, d), lambda i, *_: (jnp.minimum(i, npt - 1), 0)),
                   pl.BlockSpec((ROW_TILE, d), lambda i, *_: (jnp.maximum(i - npt, 0), 0))],
        scratch_shapes=[pltpu.VMEM((2, TOP_K, ROW_TILE, d), F32), pltpu.SemaphoreType.DMA((2,))],
    )
    return pl.pallas_call(
        functools.partial(_combine_kernel, n_prompt_tiles=npt),
        grid_spec=grid_spec,
        out_shape=[jax.ShapeDtypeStruct((n_prompt, d), F32), jax.ShapeDtypeStruct((n_sample, d), F32)],
        compiler_params=_params(("arbitrary",)),
        name="moe_combine",
    )(dest_flat, y_rows, probs, h1, nw)


def _row_tile(n, cap, mult):
    return max(t for t in range(mult, min(n, cap) + 1, mult) if n % t == 0)


def _rope_tables(pos0, t, dk):
    inv = 1.0 / (ROPE_BASE ** jnp.linspace(0.0, 1.0, dk // 2, dtype=F32))
    ang = (pos0 + jnp.arange(t, dtype=F32))[:, None] * inv[None, :]
    cos = jnp.repeat(jnp.cos(ang), 2, axis=-1)
    sin = jnp.repeat(jnp.sin(ang), 2, axis=-1)
    sign = jnp.where(jnp.arange(dk) % 2 == 0, -1.0, 1.0).astype(F32)
    return cos, sin * sign


def kernel(x_prompt, x_sample, state_gla, state_ret, norm_mix, w_in, gla_w_gk, gla_b_gk, gla_norm_w, w_out,
           norm_ffn, router_w, router_b, w_gate, b_gate, w_up, b_up, w_down, b_down, norm_final):
    bp, tp, d = x_prompt.shape
    bs, ts, _ = x_sample.shape
    assert ts == 1 and w_in.shape[0] == 1
    dk, dv = state_gla.shape[-2], state_gla.shape[-1]
    qk, vv = HEADS * dk, HEADS * dv
    n_p, n_s = bp * tp, bs * ts
    n = n_p + n_s
    ne = router_w.shape[-1]
    assert n_p % ROW_TILE == 0 and n_s % ROW_TILE == 0

    c_rank = 2 * qk + 2 * vv
    w_gk = jnp.pad(gla_w_gk[0], ((0, LANE - GLA_RANK), (0, 0))).astype(BF16)
    b_gk = gla_b_gk[0][None, :]
    cols = (0, qk, 2 * qk, 2 * qk + vv, c_rank, c_rank + qk, c_rank + 2 * qk, c_rank + 2 * qk + vv)
    nmix = norm_mix[0][None, :]
    wo = w_out[0].astype(BF16)
    rwt = router_w[0].T.astype(BF16)
    rb = router_b[0][:, None]
    nffn = norm_ffn[0][None, :]
    gnw = gla_norm_w[0][None, :]

    xp = x_prompt.reshape(n_p, d)
    xs = x_sample.reshape(n_s, d)
    h_lg = _norm_gate(xp, nmix, w_in[0], c_rank, w_gk, b_gk, _row_tile(n_p, 512, 16), n, 0)
    h, lg = _norm_gate(xs, nmix, w_in[0], c_rank, w_gk, b_gk, n_s, n, n_p // n_s, prev=h_lg)
    proj = _in_proj(h, w_in[0], c_rank, tm=_row_tile(n, 1088, 16), tn=512)

    cos_p, sin_p = _rope_tables(0.0, tp, dk)
    cos_s, sin_s = _rope_tables(float(PAST_LEN), 1, dk)
    og_p, st_gla_p, og_s, or_s, s_gla_s, s_ret_s = _gla_step(
        proj, lg, gnw, cos_s, sin_s, state_gla[0], state_ret[0], bp, tp, n_p // n_s, cols)
    or_p, s_ret_p = _ret_prompt(proj, cos_p, sin_p, bp, tp, dk, dv, cols[4], cols[5], cols[6], cols[7])
    s_gla_p = jnp.swapaxes(st_gla_p, -1, -2)

    tm_p = min(256, n_p)
    h1, lt_p = _out_proj(og_p, or_p, xp, wo, nffn, rwt, rb, tm_p, n, 0)
    h1, lt_s = _out_proj(og_s, or_s, xs, wo, nffn, rwt, rb, n_s, n, n_p // n_s, h1_prev=h1)
    logits_t = jnp.concatenate([lt_p, lt_s], axis=1)

    n_slots = (n * TOP_K) // SLOT_ROWS + ne
    n_tiles_max = (n * TOP_K) // MM_ROWS + ne
    dest, probs_t, item_meta, tile_meta = _route(logits_t, n_tiles_max)
    pad_lo = item_meta[4, :ne]
    pad_hi = item_meta[5, :ne]
    dest_flat = dest.reshape(-1)
    experts = item_meta[0, :n_slots]
    nsub = item_meta[1, :n_slots]
    counts = item_meta[2:4, 0]
    tile_ids = tile_meta[0, :n_tiles_max]

    xs_rows = _gather(dest_flat, tile_ids, counts, pad_lo, pad_hi, h1, nffn, n_slots, n_tiles_max)
    hid = _moe_up(experts, nsub, counts, xs_rows.reshape(n_slots, SLOT_ROWS, d), w_gate[0], w_up[0],
                  b_gate[0][:, None, :], b_up[0][:, None, :], tn=256)
    y_rows = _moe_down(experts, nsub, counts, hid, w_down[0], b_down[0][:, None, :], tn=512)

    y_p, y_s = _combine(dest_flat, y_rows.reshape(n_slots * SLOT_ROWS, d), probs_t.T, h1,
                        norm_final[None, :], n_p, n_s)
    return (y_p.reshape(bp, tp, d), y_s.reshape(bs, ts, d),
            s_gla_p[None], s_ret_p[None], s_gla_s[None], s_ret_s[None])
```

```python
import functools
import math

import numpy as np
import jax
import jax.numpy as jnp
from jax import lax
from jax.experimental import pallas as pl
from jax.experimental.pallas import tpu as pltpu

F32 = jnp.float32
BF16 = jnp.bfloat16
I32 = jnp.int32

HEADS = 4
GLA_RANK = 16
GLA_GATE_NORM = 16.0
ROPE_BASE = 10000.0
TOP_K = 4
SWIGLU_LIMIT = 7.0
SWIGLU_ALPHA = 1.702
EPS = 1e-6
PAST_LEN = 16384

LANE = 128
ROW_TILE = 128
SLOT_TILES = 10
SLOT_ROWS = SLOT_TILES * ROW_TILE
MM_ROWS = 2 * ROW_TILE
SLOT_CHUNKS = SLOT_ROWS // MM_ROWS
GLA_CHUNK = 64
GLA_SUB = 16
RET_CHUNK = 256
VMEM_LIMIT = 60 * 1024 * 1024

_NT = (((1,), (1,)), ((), ()))
_TN = (((0,), (0,)), ((), ()))


def _params(sem, vmem=VMEM_LIMIT):
    return pltpu.CompilerParams(dimension_semantics=sem, vmem_limit_bytes=vmem)


def _sigmoid(x):
    return 1.0 / (1.0 + jnp.exp(-x))


def _rms_scale(x):
    return lax.rsqrt(jnp.mean(x * x, axis=-1, keepdims=True) + EPS)


def _norm_gate_kernel(x_ref, nw_ref, wr_ref, wgk_ref, bgk_ref, *rest):
    h_ref, lg_ref = rest[-2], rest[-1]
    x = x_ref[...]
    hb = (x * _rms_scale(x) * nw_ref[...]).astype(BF16)
    h_ref[...] = hb
    gr = jnp.dot(hb, wr_ref[...].astype(BF16), preferred_element_type=F32)
    z = jnp.dot(gr.astype(BF16), wgk_ref[...], preferred_element_type=F32) + bgk_ref[...]
    log_sig = jnp.minimum(z, 0.0) - jnp.log1p(jnp.exp(-jnp.abs(z)))
    lg_ref[...] = log_sig * (1.0 / GLA_GATE_NORM)


def _norm_gate(x, nw, w_in, rank_col, wgk, bgk, tm, n_total, row_block0, prev=None):
    m, d = x.shape
    assert m % tm == 0 and rank_col % LANE == 0
    nlg = wgk.shape[1]
    in_specs = [
        pl.BlockSpec((tm, d), lambda i: (i, 0)),
        pl.BlockSpec((1, d), lambda i: (0, 0)),
        pl.BlockSpec((None, d, LANE), lambda i: (0, 0, rank_col // LANE)),
        pl.BlockSpec((LANE, nlg), lambda i: (0, 0)),
        pl.BlockSpec((1, nlg), lambda i: (0, 0)),
    ]
    args = [x, nw, w_in, wgk, bgk]
    aliases = {}
    if prev is not None:
        in_specs += [pl.BlockSpec(memory_space=pl.ANY)] * 2
        args += list(prev)
        aliases = {len(args) - 2: 0, len(args) - 1: 1}
    return pl.pallas_call(
        _norm_gate_kernel,
        grid=(m // tm,),
        in_specs=in_specs,
        out_specs=[pl.BlockSpec((tm, d), lambda i: (row_block0 + i, 0)),
                   pl.BlockSpec((tm, nlg), lambda i: (row_block0 + i, 0))],
        out_shape=[jax.ShapeDtypeStruct((n_total, d), BF16), jax.ShapeDtypeStruct((n_total, nlg), F32)],
        input_output_aliases=aliases,
        compiler_params=_params(("parallel",)),
        name="norm_gate",
    )(*args)


def _in_proj_kernel(h_ref, wm_ref, wx_ref, proj_ref, w_scr, *, first_shifted):
    j = pl.program_id(0)
    tn = wm_ref.shape[1]

    first = pl.program_id(1) == 0

    def project():
        proj_ref[...] = jnp.dot(h_ref[...], w_scr[...], preferred_element_type=F32)

    @pl.when(jnp.logical_and(first, j < first_shifted))
    def _():
        w_scr[...] = wm_ref[...].astype(BF16)
        project()

    @pl.when(jnp.logical_and(first, j >= first_shifted))
    def _():
        w = jnp.concatenate([wm_ref[...], wx_ref[...]], axis=1)
        w_scr[...] = w[:, GLA_RANK:GLA_RANK + tn].astype(BF16)
        project()

    @pl.when(jnp.logical_not(first))
    def _():
        project()


def _in_proj(h, w_in, rank_col, tm, tn):
    m, d = h.shape
    n = w_in.shape[2] - GLA_RANK
    assert m % tm == 0 and n % tn == 0 and rank_col % tn == 0 and tn % LANE == 0
    per = tn // LANE
    return pl.pallas_call(
        functools.partial(_in_proj_kernel, first_shifted=rank_col // tn),
        grid=(n // tn, m // tm),
        in_specs=[
            pl.BlockSpec((tm, d), lambda j, i: (i, 0)),
            pl.BlockSpec((None, d, tn), lambda j, i: (0, 0, j)),
            pl.BlockSpec((None, d, LANE), lambda j, i: (0, 0, (j + 1) * per)),
        ],
        out_specs=pl.BlockSpec((tm, tn), lambda j, i: (i, j)),
        out_shape=jax.ShapeDtypeStruct((m, n), F32),
        scratch_shapes=[pltpu.VMEM((d, tn), BF16)],
        compiler_params=_params(("arbitrary", "arbitrary")),
        name="in_proj",
    )(h, w_in, w_in)


def _rope(x, cos, sin_signed, even):
    n = x.shape[-1]
    nxt = pltpu.roll(x, n - 1, axis=x.ndim - 1)
    prv = pltpu.roll(x, 1, axis=x.ndim - 1)
    return x * cos + jnp.where(even, nxt, prv) * sin_signed


def _ret_prompt_kernel(q_ref, k_ref, v_ref, g_ref, cos_ref, sin_ref, o_ref, s_ref):
    c = q_ref.shape[0]
    dk = q_ref.shape[1] // HEADS
    dv = v_ref.shape[1] // HEADS

    @pl.when(pl.program_id(1) == 0)
    def _():
        s_ref[...] = jnp.zeros_like(s_ref)

    ti = lax.broadcasted_iota(I32, (c, c), 0)
    si = lax.broadcasted_iota(I32, (c, c), 1)
    dlt = (ti - si).astype(F32)
    causal = ti >= si
    tcol = lax.broadcasted_iota(I32, (c, 1), 0).astype(F32)
    even = (lax.broadcasted_iota(I32, (c, dk), 1) % 2) == 0
    cos = cos_ref[...]
    sin = sin_ref[...]
    for h in range(HEADS):
        lgam = math.log(1.0 - 2.0 ** (-5.0 - h))
        q = _rope(q_ref[:, h * dk:(h + 1) * dk], cos, sin, even)
        k = _rope(k_ref[:, h * dk:(h + 1) * dk], cos, sin, even) * (dk ** -0.5)
        qb = q.astype(BF16)
        vb = v_ref[:, h * dv:(h + 1) * dv].astype(BF16)
        decay = jnp.where(causal, jnp.exp(dlt * lgam), 0.0)
        a = lax.dot_general(qb, k.astype(BF16), _NT, preferred_element_type=F32) * decay
        s = s_ref[0, h]
        o = jnp.dot(a.astype(BF16), vb, preferred_element_type=F32)
        o = o + jnp.dot(qb, s.astype(BF16), preferred_element_type=F32) * jnp.exp((tcol + 1.0) * lgam)
        kd = (k * jnp.exp((c - 1.0 - tcol) * lgam)).astype(BF16)
        s_ref[0, h] = math.exp(c * lgam) * s + lax.dot_general(kd, vb, _TN, preferred_element_type=F32)
        g = g_ref[:, h * dv:(h + 1) * dv]
        o_ref[:, h * dv:(h + 1) * dv] = (o * _rms_scale(o) * (g * _sigmoid(g))).astype(BF16)


def _ret_prompt(proj, cos, sin, batch, seq, dk, dv, col_q, col_k, col_v, col_g):
    c = math.gcd(seq, RET_CHUNK)
    nt = seq // c
    qw, vw = HEADS * dk, HEADS * dv
    return pl.pallas_call(
        _ret_prompt_kernel,
        grid=(batch, nt),
        in_specs=[
            pl.BlockSpec((c, qw), lambda b, t: (b * nt + t, col_q // qw)),
            pl.BlockSpec((c, qw), lambda b, t: (b * nt + t, col_k // qw)),
            pl.BlockSpec((c, vw), lambda b, t: (b * nt + t, col_v // vw)),
            pl.BlockSpec((c, vw), lambda b, t: (b * nt + t, col_g // vw)),
            pl.BlockSpec((c, dk), lambda b, t: (t, 0)),
            pl.BlockSpec((c, dk), lambda b, t: (t, 0)),
        ],
        out_specs=[
            pl.BlockSpec((c, vw), lambda b, t: (b * nt + t, 0)),
            pl.BlockSpec((1, HEADS, dk, dv), lambda b, t: (b, 0, 0, 0)),
        ],
        out_shape=[jax.ShapeDtypeStruct((batch * seq, vw), BF16),
                   jax.ShapeDtypeStruct((batch, HEADS, dk, dv), F32)],
        compiler_params=_params(("parallel", "arbitrary")),
        name="ret_prompt",
    )(proj, proj, proj, proj, cos, sin)


def _gla_chunk(q_ref, k_ref, v_ref, g_ref, lg_ref, nw_ref, o_ref, st_ref, b_scr, a_scr):
    c = q_ref.shape[0]
    dk = q_ref.shape[1] // HEADS
    dv = v_ref.shape[1] // HEADS
    sub = GLA_SUB
    nsub = c // sub

    @pl.when(pl.program_id(1) == 0)
    def _():
        st_ref[...] = jnp.zeros_like(st_ref)

    ri = lax.broadcasted_iota(I32, (c, c), 0)
    ci = lax.broadcasted_iota(I32, (c, c), 1)
    lower = (ri >= ci).astype(BF16)
    cis = lax.broadcasted_iota(I32, (sub, c), 1)

    for h in range(HEADS):
        lg = lg_ref[:, h * dk:(h + 1) * dk]
        l1 = lg.astype(BF16)
        r1 = lg - l1.astype(F32)
        l2 = r1.astype(BF16)
        l3 = (r1 - l2.astype(F32)).astype(BF16)
        b_scr[h] = (jnp.dot(lower, l1, preferred_element_type=F32)
                    + jnp.dot(lower, l2, preferred_element_type=F32)
                    + jnp.dot(lower, l3, preferred_element_type=F32))

    def sub_block(i, carry):
        r0 = pl.multiple_of(i * sub, sub)
        rprev = jnp.maximum(r0 - 1, 0)
        for h in range(HEADS):
            hs = slice(h * dk, (h + 1) * dk)
            qi = q_ref[pl.ds(r0, sub), hs] * (dk ** -0.5)
            ki = k_ref[pl.ds(r0, sub), hs]
            bi = b_scr[h, pl.ds(r0, sub), :]
            ad = jnp.zeros((sub, c), F32)
            for s in range(sub):
                e = jnp.exp(jnp.minimum(bi - bi[s:s + 1, :], 0.0))
                col = jnp.sum(qi * e * ki[s:s + 1, :], axis=1, keepdims=True)
                ad = jnp.where(cis == r0 + s, col, ad)
            bref = b_scr[h, pl.ds(rprev, 1), :]
            qq = (qi * jnp.exp(jnp.minimum(bi - bref, 0.0))).astype(BF16)
            kk = (k_ref[:, hs] * jnp.exp(jnp.minimum(bref - b_scr[h], 0.0))).astype(BF16)
            ao = lax.dot_general(qq, kk, _NT, preferred_element_type=F32)
            a_scr[h, pl.ds(r0, sub), :] = jnp.where(cis < r0, ao, ad)
        return carry

    lax.fori_loop(0, nsub, sub_block, 0)

    for h in range(HEADS):
        hs = slice(h * dk, (h + 1) * dk)
        vs = slice(h * dv, (h + 1) * dv)
        b = b_scr[h]
        b_last = b[c - 1:c, :]
        vb = v_ref[:, vs].astype(BF16)
        qe = (q_ref[:, hs] * (dk ** -0.5) * jnp.exp(b)).astype(BF16)
        kd = (k_ref[:, hs] * jnp.exp(b_last - b)).astype(BF16)
        st = st_ref[0, h]
        a = jnp.where(ri >= ci, a_scr[h], 0.0)
        o = jnp.dot(a.astype(BF16), vb, preferred_element_type=F32)
        o = o + lax.dot_general(qe, st.astype(BF16), _NT, preferred_element_type=F32)
        st_ref[0, h] = st * jnp.exp(b_last) + lax.dot_general(vb, kd, _TN, preferred_element_type=F32)
        g = g_ref[:, vs]
        o_ref[:, vs] = (o * _rms_scale(o) * nw_ref[...] * (g * _sigmoid(g))).astype(BF16)


def _to_col(row, eye):
    return jnp.sum(jnp.where(eye, row, 0.0), axis=1, keepdims=True)


def _step_rows(b, seq, proj_ref, lg_ref, cos_ref, sin_ref, nw_ref, sg_ref, sr_ref,
               og_ref, or_ref, ng_ref, nr_ref, cols):
    dk, dv = sg_ref.shape[-2], sg_ref.shape[-1]
    col_gq, col_gk, col_gv, col_gg, col_rq, col_rk, col_rv, col_rg = cols
    eye = lax.broadcasted_iota(I32, (dk, dk), 0) == lax.broadcasted_iota(I32, (dk, dk), 1)
    even = (lax.broadcasted_iota(I32, (1, dk), 1) % 2) == 0
    row = pl.ds(b, 1)

    def finish(o, g, w):
        o = o * _rms_scale(o)
        if w is not None:
            o = o * w
        return o * (g * _sigmoid(g))

    for h in range(HEADS):
        q = proj_ref[row, col_gq + h * dk:col_gq + (h + 1) * dk] * (dk ** -0.5)
        k = proj_ref[row, col_gk + h * dk:col_gk + (h + 1) * dk]
        v = proj_ref[row, col_gv + h * dv:col_gv + (h + 1) * dv]
        g = proj_ref[row, col_gg + h * dv:col_gg + (h + 1) * dv]
        a = jnp.exp(lg_ref[row, h * dk:(h + 1) * dk])
        s_new = _to_col(a, eye) * sg_ref[seq, h] + _to_col(k, eye) * v
        ng_ref[seq, h] = s_new
        o = jnp.sum(_to_col(q, eye) * s_new, axis=0, keepdims=True)
        og_ref[row, h * dv:(h + 1) * dv] = finish(o, g, nw_ref[...])

        gamma = 1.0 - 2.0 ** (-5.0 - h)
        q = _rope(proj_ref[row, col_rq + h * dk:col_rq + (h + 1) * dk], cos_ref[...], sin_ref[...], even)
        k = _rope(proj_ref[row, col_rk + h * dk:col_rk + (h + 1) * dk], cos_ref[...], sin_ref[...], even)
        k = k * (dk ** -0.5)
        v = proj_ref[row, col_rv + h * dv:col_rv + (h + 1) * dv]
        g = proj_ref[row, col_rg + h * dv:col_rg + (h + 1) * dv]
        s_new = gamma * sr_ref[seq, h] + _to_col(k, eye) * v
        nr_ref[seq, h] = s_new
        o = jnp.sum(_to_col(q, eye) * s_new, axis=0, keepdims=True)
        or_ref[row, h * dv:(h + 1) * dv] = finish(o, g, None)


def _gla_step_kernel(q_ref, k_ref, v_ref, g_ref, lg_ref, nw_ref, sproj_ref, slg_ref, cos_ref, sin_ref,
                     sg_ref, sr_ref, o_ref, st_ref, og_ref, or_ref, ng_ref, nr_ref, b_scr, a_scr,
                     *, nt, cols):
    _gla_chunk(q_ref, k_ref, v_ref, g_ref, lg_ref, nw_ref, o_ref, st_ref, b_scr, a_scr)
    per_step = sg_ref.shape[0]
    first = (pl.program_id(0) * nt + pl.program_id(1)) * per_step
    for s in range(per_step):
        _step_rows(first + s, s, sproj_ref, slg_ref, cos_ref, sin_ref, nw_ref, sg_ref, sr_ref,
                   og_ref, or_ref, ng_ref, nr_ref, cols)


def _gla_step(proj, lg, nw, cos_s, sin_s, s_gla, s_ret, batch, seq, sample_row_block, cols):
    nb, _, dk, dv = s_gla.shape
    c = math.gcd(seq, GLA_CHUNK)
    nt = seq // c
    assert nb % (batch * nt) == 0
    per_step = nb // (batch * nt)
    qw, vw = HEADS * dk, HEADS * dv
    ncol = proj.shape[1]
    col_q, col_k, col_v, col_g = cols[:4]
    blk = lambda width, col: pl.BlockSpec((c, width), lambda b, t: (b * nt + t, col // width))
    full = lambda shape: pl.BlockSpec(shape, lambda b, t: (0,) * len(shape))
    rows = lambda width: pl.BlockSpec((nb, width), lambda b, t: (sample_row_block, 0))
    state = pl.BlockSpec((per_step, HEADS, dk, dv), lambda b, t: (b * nt + t, 0, 0, 0))
    return pl.pallas_call(
        functools.partial(_gla_step_kernel, nt=nt, cols=cols),
        grid=(batch, nt),
        in_specs=[blk(qw, col_q), blk(qw, col_k), blk(vw, col_v), blk(vw, col_g), blk(qw, 0), full((1, dv)),
                  rows(ncol), rows(qw), full((1, dk)), full((1, dk)), state, state],
        out_specs=[
            pl.BlockSpec((c, vw), lambda b, t: (b * nt + t, 0)),
            pl.BlockSpec((1, HEADS, dv, dk), lambda b, t: (b, 0, 0, 0)),
            full((nb, vw)), full((nb, vw)), state, state,
        ],
        out_shape=[jax.ShapeDtypeStruct((batch * seq, vw), BF16),
                   jax.ShapeDtypeStruct((batch, HEADS, dv, dk), F32),
                   jax.ShapeDtypeStruct((nb, vw), F32), jax.ShapeDtypeStruct((nb, vw), F32),
                   jax.ShapeDtypeStruct(s_gla.shape, F32), jax.ShapeDtypeStruct(s_ret.shape, F32)],
        scratch_shapes=[pltpu.VMEM((HEADS, c, dk), F32), pltpu.VMEM((HEADS, c, c), F32)],
        compiler_params=_params(("arbitrary", "arbitrary")),
        name="gla_step",
    )(proj, proj, proj, proj, lg, nw, proj, lg, cos_s, sin_s, s_gla, s_ret)


def _out_proj_kernel(og_ref, or_ref, x_ref, wg_ref, wr_ref, nw_ref, rwt_ref, rb_ref, *rest):
    h1_ref, lt_ref = rest[-2], rest[-1]
    acc = jnp.dot(og_ref[...].astype(BF16), wg_ref[...], preferred_element_type=F32)
    acc = acc + jnp.dot(or_ref[...].astype(BF16), wr_ref[...], preferred_element_type=F32)
    h1 = x_ref[...] + acc
    h1_ref[...] = h1
    flat = (h1 * _rms_scale(h1) * nw_ref[...]).astype(BF16)
    lt_ref[...] = lax.dot_general(rwt_ref[...], flat, _NT, preferred_element_type=F32) + rb_ref[...]


def _out_proj(og, orr, x, wo, nw, rwt, rb, tm, n_total, row_block0, h1_prev=None):
    m, d = x.shape
    half = og.shape[1]
    ne = rwt.shape[0]
    const = lambda shape: pl.BlockSpec(shape, lambda i: (0,) * len(shape), pipeline_mode=pl.Buffered(1))
    in_specs = [
        pl.BlockSpec((tm, half), lambda i: (i, 0)),
        pl.BlockSpec((tm, half), lambda i: (i, 0)),
        pl.BlockSpec((tm, d), lambda i: (i, 0)),
        const((half, d)), pl.BlockSpec((half, d), lambda i: (1, 0), pipeline_mode=pl.Buffered(1)),
        const((1, d)), const((ne, d)), const((ne, 1)),
    ]
    args = [og, orr, x, wo, wo, nw, rwt, rb]
    aliases = {}
    if h1_prev is not None:
        in_specs.append(pl.BlockSpec(memory_space=pl.ANY))
        args.append(h1_prev)
        aliases = {len(args) - 1: 0}
    return pl.pallas_call(
        _out_proj_kernel,
        grid=(m // tm,),
        in_specs=in_specs,
        out_specs=[
            pl.BlockSpec((tm, d), lambda i: (row_block0 + i, 0)),
            pl.BlockSpec((ne, tm), lambda i: (0, i)),
        ],
        out_shape=[jax.ShapeDtypeStruct((n_total, d), F32), jax.ShapeDtypeStruct((ne, m), F32)],
        input_output_aliases=aliases,
        compiler_params=_params(("parallel",)),
        name="out_proj",
    )(*args)


def _route_kernel(lt_ref, dest_ref, prob_ref, item_ref, tile_ref, m_scr, pos_scr):
    ne, n = lt_ref.shape
    nblk = n // LANE
    logits = lt_ref[...]
    eio = lax.broadcasted_iota(I32, (ne, n), 0).astype(F32)
    vals, idxs = [], []
    for _ in range(TOP_K):
        m = jnp.max(logits, axis=0, keepdims=True)
        ik = jnp.min(jnp.where(logits == m, eio, float(ne)), axis=0, keepdims=True)
        vals.append(m)
        idxs.append(ik)
        logits = jnp.where(eio == ik, -jnp.inf, logits)
    ex = [jnp.exp(v - vals[0]) for v in vals]
    den = ex[0] + ex[1] + ex[2] + ex[3]
    for k in range(TOP_K):
        prob_ref[k:k + 1, :] = ex[k] / den
    sel = jnp.zeros((ne, n), F32)
    for ik in idxs:
        sel = jnp.where(eio == ik, 1.0, sel)
    m_scr[...] = sel

    upper = (lax.broadcasted_iota(I32, (LANE, LANE), 0) <= lax.broadcasted_iota(I32, (LANE, LANE), 1)).astype(BF16)

    def prefix(cb, carry):
        c0 = pl.multiple_of(cb * LANE, LANE)
        mc = m_scr[:, pl.ds(c0, LANE)]
        incl = jnp.dot(mc.astype(BF16), upper, preferred_element_type=F32)
        pos_scr[:, pl.ds(c0, LANE)] = incl - mc + carry
        return carry + jnp.sum(mc, axis=1, keepdims=True)

    cnt = lax.fori_loop(0, nblk, prefix, jnp.zeros((ne, 1), F32))

    def ceil_div(x, step, most):
        r = jnp.zeros_like(x)
        for mlt in range(most):
            r = r + jnp.where(x > float(mlt * step), 1.0, 0.0)
        return r

    cnt_l = jnp.broadcast_to(cnt, (ne, LANE))
    n_slot = ceil_div(cnt_l, SLOT_ROWS, -(-n // SLOT_ROWS))
    n_tile = ceil_div(cnt_l, MM_ROWS, -(-n // MM_ROWS))
    strict = (lax.broadcasted_iota(I32, (ne, ne), 0) > lax.broadcasted_iota(I32, (ne, ne), 1)).astype(BF16)
    slot0 = jnp.dot(strict, n_slot.astype(BF16), preferred_element_type=F32)
    tile0 = jnp.dot(strict, n_tile.astype(BF16), preferred_element_type=F32)

    destf = slot0[:, :1] * float(SLOT_ROWS) + pos_scr[...]
    for k in range(TOP_K):
        dk_ = jnp.sum(jnp.where(eio == idxs[k], destf, 0.0), axis=0, keepdims=True)
        dest_ref[k:k + 1, :] = dk_.astype(I32)

    def lookup(table, onehot):
        return jnp.sum(jnp.where(onehot, table, 0.0), axis=0, keepdims=True)

    e_col = lax.broadcasted_iota(I32, (ne, LANE), 0).astype(F32)
    it = lax.broadcasted_iota(I32, (ne, LANE), 1).astype(F32)
    n_items = jnp.sum(n_slot, axis=0, keepdims=True)
    e_it = jnp.minimum(jnp.sum(jnp.where(slot0 + n_slot <= it, 1.0, 0.0), axis=0, keepdims=True), ne - 1.0)
    oh = e_col == e_it
    rows = lookup(cnt_l, oh) - (it[:1] - lookup(slot0, oh)) * float(SLOT_ROWS)
    rows = jnp.clip(rows, 0.0, float(SLOT_ROWS))
    rows = jnp.where(it[:1] < n_items, rows, 0.0)
    item_ref[...] = jnp.zeros_like(item_ref)
    item_ref[0:1, :] = e_it.astype(I32)
    item_ref[1:2, :] = ceil_div(rows, ROW_TILE, SLOT_TILES).astype(I32)
    item_ref[2:3, :] = n_items.astype(I32)
    item_ref[3:4, :] = jnp.sum(n_tile, axis=0, keepdims=True).astype(I32)
    diag = e_col == it
    item_ref[4:5, :] = lookup(slot0 * float(SLOT_ROWS) + cnt_l, diag).astype(I32)
    item_ref[5:6, :] = lookup(slot0 * float(SLOT_ROWS) + n_tile * float(MM_ROWS), diag).astype(I32)

    ntl = tile_ref.shape[1]
    tile_ref[...] = jnp.zeros_like(tile_ref)
    for cb in range(ntl // LANE):
        gi = it + float(cb * LANE)
        e_g = jnp.minimum(jnp.sum(jnp.where(tile0 + n_tile <= gi, 1.0, 0.0), axis=0, keepdims=True), ne - 1.0)
        ohg = e_col == e_g
        tid = lookup(slot0, ohg) * float(SLOT_CHUNKS) + (gi[:1] - lookup(tile0, ohg))
        tile_ref[0:1, cb * LANE:(cb + 1) * LANE] = tid.astype(I32)


def _route(logits_t, n_tiles_max):
    ne, n = logits_t.shape
    ntl = -(-n_tiles_max // LANE) * LANE
    return pl.pallas_call(
        _route_kernel,
        out_shape=[jax.ShapeDtypeStruct((TOP_K, n), I32), jax.ShapeDtypeStruct((TOP_K, n), F32),
                   jax.ShapeDtypeStruct((8, LANE), I32), jax.ShapeDtypeStruct((8, ntl), I32)],
        scratch_shapes=[pltpu.VMEM((ne, n), F32), pltpu.VMEM((ne, n), F32)],
        compiler_params=pltpu.CompilerParams(vmem_limit_bytes=VMEM_LIMIT),
        name="route",
    )(logits_t)


def _gather_kernel(dest_sm, tile_sm, cnt_sm, pad_lo_sm, pad_hi_sm, h1_hbm, nw_ref, xs_ref, inv_sm, buf, sem):
    g = pl.program_id(0)
    n_tiles = cnt_sm[1]
    n_tok = dest_sm.shape[0] // TOP_K
    group = 16

    def issue(gi, slot):
        base = tile_sm[gi] * MM_ROWS

        def body(r, carry):
            src = inv_sm[base + r]
            pltpu.make_async_copy(h1_hbm.at[pl.ds(src, 1)], buf.at[slot, pl.ds(r, 1)], sem.at[slot]).start()
            return carry

        lax.fori_loop(0, MM_ROWS, body, 0, unroll=8)

    @pl.when(g == 0)
    def _():
        spread = (1 << (n_tok.bit_length() - 1)) - 1

        def pad_expert(e, carry):
            def clear(i, c2):
                inv_sm[i] = i & spread
                return c2

            lax.fori_loop(pad_lo_sm[e], pad_hi_sm[e], clear, 0)
            return carry

        lax.fori_loop(0, pad_lo_sm.shape[0], pad_expert, 0)
        for k in range(TOP_K):
            def scatter(t, carry):
                inv_sm[dest_sm[k * n_tok + t]] = t
                return carry

            lax.fori_loop(0, n_tok, scatter, 0, unroll=8)
        issue(0, 0)

    @pl.when(g + 1 < n_tiles)
    def _():
        issue(g + 1, (g + 1) % 2)

    @pl.when(g < n_tiles)
    def _():
        slot = g % 2
        pltpu.make_async_copy(h1_hbm.at[pl.ds(0, MM_ROWS)], buf.at[slot], sem.at[slot]).wait()

        def norm_rows(r, carry):
            r0 = pl.multiple_of(r * group, group)
            x = buf[slot, pl.ds(r0, group), :]
            xs_ref[pl.ds(r0, group), :] = (x * _rms_scale(x) * nw_ref[...]).astype(BF16)
            return carry

        lax.fori_loop(0, MM_ROWS // group, norm_rows, 0, unroll=4)


def _gather(dest_flat, tile_ids, counts, pad_lo, pad_hi, h1, nw, n_slots, n_tiles_max):
    d = h1.shape[1]

    def out_map(g, dest_sm, tile_sm, cnt_sm, *_):
        return (tile_sm[jnp.minimum(g, cnt_sm[1] - 1)], 0)

    grid_spec = pltpu.PrefetchScalarGridSpec(
        num_scalar_prefetch=5,
        grid=(n_tiles_max,),
        in_specs=[pl.BlockSpec(memory_space=pl.ANY),
                  pl.BlockSpec((1, d), lambda g, *_: (0, 0))],
        out_specs=pl.BlockSpec((MM_ROWS, d), out_map),
        scratch_shapes=[pltpu.SMEM((n_slots * SLOT_ROWS,), I32),
                        pltpu.VMEM((2, MM_ROWS, d), F32),
                        pltpu.SemaphoreType.DMA((2,))],
    )
    return pl.pallas_call(
        _gather_kernel,
        grid_spec=grid_spec,
        out_shape=jax.ShapeDtypeStruct((n_slots * SLOT_ROWS, d), BF16),
        compiler_params=_params(("arbitrary",)),
        name="moe_gather",
    )(dest_flat, tile_ids, counts, pad_lo, pad_hi, h1, nw)


def _for_row_chunks(ns, matmul, finish, clear, acc_scr):
    acc_a, acc_b = acc_scr
    odd = jnp.logical_and(ns > 1, (ns & 1) == 1)
    n_full = jnp.maximum(lax.shift_right_logical(ns, 1), 1)
    n_pair = lax.shift_right_logical(n_full - 1, 1)
    row0 = lambda c: pl.multiple_of(c * MM_ROWS, MM_ROWS)
    acc_a[...] = matmul(0, MM_ROWS)

    def pair(p, carry):
        c = 2 * p
        acc_b[...] = matmul(row0(c + 1), MM_ROWS)
        finish(row0(c), MM_ROWS, acc_a[...])
        acc_a[...] = matmul(row0(c + 2), MM_ROWS)
        finish(row0(c + 1), MM_ROWS, acc_b[...])
        return carry

    lax.fori_loop(0, n_pair, pair, 0)
    done = 2 * n_pair
    extra = (n_full - 1) > done
    t0 = pl.multiple_of((ns - 1) * ROW_TILE, ROW_TILE)

    @pl.when(extra)
    def _():
        acc_b[...] = matmul(row0(done + 1), MM_ROWS)
        finish(row0(done), MM_ROWS, acc_a[...])

    def drain(acc, c):
        @pl.when(odd)
        def _():
            tail = matmul(t0, ROW_TILE)
            finish(row0(c), MM_ROWS, acc[...])
            finish(t0, ROW_TILE, tail)

        @pl.when(jnp.logical_not(odd))
        def _():
            finish(row0(c), MM_ROWS, acc[...])

    @pl.when(extra)
    def _():
        drain(acc_b, done + 1)

    @pl.when(jnp.logical_not(extra))
    def _():
        drain(acc_a, done)

    def clear_tile(s, carry):
        clear(pl.multiple_of(s * ROW_TILE, ROW_TILE))
        return carry

    lax.fori_loop(jnp.maximum(ns, MM_ROWS // ROW_TILE), SLOT_TILES, clear_tile, 0)


def _moe_up_kernel(exp_sm, nsub_sm, cnt_sm, x_ref, wg_ref, wu_ref, bg_ref, bu_ref, hid_ref, w_scr, *acc_scr):
    it = pl.program_id(0)
    tn = wg_ref.shape[2]

    @pl.when(it < cnt_sm[0])
    def _():
        w_scr[:, :tn] = wg_ref[0].astype(BF16)
        w_scr[:, tn:] = wu_ref[0].astype(BF16)

        def matmul(r0, rows):
            return jnp.dot(x_ref[0, pl.ds(r0, rows), :], w_scr[...], preferred_element_type=F32)

        def finish(r0, rows, gu):
            g = jnp.minimum(gu[:, :tn] + bg_ref[0], SWIGLU_LIMIT)
            u = jnp.clip(gu[:, tn:] + bu_ref[0], -SWIGLU_LIMIT, SWIGLU_LIMIT)
            hid = (u + 1.0) * g * _sigmoid(SWIGLU_ALPHA * g)
            hid_ref[0, pl.ds(r0, rows), :] = hid.astype(BF16)

        def clear(r0):
            hid_ref[0, pl.ds(r0, ROW_TILE), :] = jnp.zeros((ROW_TILE, tn), BF16)

        _for_row_chunks(nsub_sm[it], matmul, finish, clear, acc_scr)


def _moe_down_kernel(exp_sm, nsub_sm, cnt_sm, h_ref, w_ref, b_ref, y_ref, w_scr, *acc_scr):
    it = pl.program_id(0)
    tn = w_ref.shape[2]

    @pl.when(it < cnt_sm[0])
    def _():
        w_scr[...] = w_ref[0].astype(BF16)

        def matmul(r0, rows):
            return jnp.dot(h_ref[0, pl.ds(r0, rows), :], w_scr[...], preferred_element_type=F32)

        def finish(r0, rows, y):
            y_ref[0, pl.ds(r0, rows), :] = y + b_ref[0]

        def clear(r0):
            y_ref[0, pl.ds(r0, ROW_TILE), :] = jnp.zeros((ROW_TILE, tn), F32)

        _for_row_chunks(nsub_sm[it], matmul, finish, clear, acc_scr)


def _item_maps(nj):
    def eff(it, j, cnt_sm):
        last = cnt_sm[0] - 1
        return jnp.minimum(it, last), jnp.where(it <= last, j, nj - 1)

    def x_map(it, j, exp_sm, nsub_sm, cnt_sm):
        ie, _ = eff(it, j, cnt_sm)
        return (ie, 0, 0)

    def w_map(it, j, exp_sm, nsub_sm, cnt_sm):
        ie, je = eff(it, j, cnt_sm)
        return (exp_sm[ie], 0, je)

    def o_map(it, j, exp_sm, nsub_sm, cnt_sm):
        ie, je = eff(it, j, cnt_sm)
        return (ie, 0, je)

    return x_map, w_map, o_map


def _moe_up(experts, nsub, counts, xs, w_gate, w_up, b_gate, b_up, tn):
    n_slots, rows, d = xs.shape
    dff = w_gate.shape[2]
    nj = dff // tn
    x_map, w_map, o_map = _item_maps(nj)
    grid_spec = pltpu.PrefetchScalarGridSpec(
        num_scalar_prefetch=3,
        grid=(counts[0], nj),
        in_specs=[pl.BlockSpec((1, rows, d), x_map),
                  pl.BlockSpec((1, d, tn), w_map), pl.BlockSpec((1, d, tn), w_map),
                  pl.BlockSpec((1, 1, tn), w_map), pl.BlockSpec((1, 1, tn), w_map)],
        out_specs=pl.BlockSpec((1, rows, tn), o_map),
        scratch_shapes=[pltpu.VMEM((d, 2 * tn), BF16),
                        pltpu.VMEM((MM_ROWS, 2 * tn), F32), pltpu.VMEM((MM_ROWS, 2 * tn), F32)],
    )
    return pl.pallas_call(
        _moe_up_kernel,
        grid_spec=grid_spec,
        out_shape=jax.ShapeDtypeStruct((n_slots, rows, dff), BF16),
        compiler_params=_params(("arbitrary", "arbitrary")),
        name="moe_up",
    )(experts, nsub, counts, xs, w_gate, w_up, b_gate, b_up)


def _moe_down(experts, nsub, counts, hid, w_down, b_down, tn):
    n_slots, rows, dff = hid.shape
    d = w_down.shape[2]
    nj = d // tn
    x_map, w_map, o_map = _item_maps(nj)
    grid_spec = pltpu.PrefetchScalarGridSpec(
        num_scalar_prefetch=3,
        grid=(counts[0], nj),
        in_specs=[pl.BlockSpec((1, rows, dff), x_map),
                  pl.BlockSpec((1, dff, tn), w_map),
                  pl.BlockSpec((1, 1, tn), w_map)],
        out_specs=pl.BlockSpec((1, rows, tn), o_map),
        scratch_shapes=[pltpu.VMEM((dff, tn), BF16),
                        pltpu.VMEM((MM_ROWS, tn), F32), pltpu.VMEM((MM_ROWS, tn), F32)],
    )
    return pl.pallas_call(
        _moe_down_kernel,
        grid_spec=grid_spec,
        out_shape=jax.ShapeDtypeStruct((n_slots, rows, d), F32),
        compiler_params=_params(("arbitrary", "arbitrary")),
        name="moe_down",
    )(experts, nsub, counts, hid, w_down, b_down)


def _combine_kernel(dest_sm, y_hbm, p_ref, h1_ref, nw_ref, yp_ref, ys_ref, buf, sem, *, n_prompt_tiles):
    i = pl.program_id(0)
    n_steps = pl.num_programs(0)
    n_tok = dest_sm.shape[0] // TOP_K

    def issue(ti, slot):
        for k in range(TOP_K):
            def body(r, carry):
                src = dest_sm[k * n_tok + ti * ROW_TILE + r]
                pltpu.make_async_copy(y_hbm.at[pl.ds(src, 1)], buf.at[slot, k, pl.ds(r, 1)], sem.at[slot]).start()
                return carry

            lax.fori_loop(0, ROW_TILE, body, 0, unroll=8)

    @pl.when(i == 0)
    def _():
        issue(0, 0)

    @pl.when(i + 1 < n_steps)
    def _():
        issue(i + 1, (i + 1) % 2)

    slot = i % 2
    for k in range(TOP_K):
        pltpu.make_async_copy(y_hbm.at[pl.ds(0, ROW_TILE)], buf.at[slot, k], sem.at[slot]).wait()
    group = 16

    def combine_into(out_ref):
        def rows(r, carry):
            r0 = pl.multiple_of(r * group, group)
            acc = h1_ref[pl.ds(r0, group), :]
            for k in range(TOP_K):
                acc = acc + p_ref[pl.ds(r0, group), k:k + 1] * buf[slot, k, pl.ds(r0, group), :]
            out_ref[pl.ds(r0, group), :] = acc * _rms_scale(acc) * nw_ref[...]
            return carry

        lax.fori_loop(0, ROW_TILE // group, rows, 0, unroll=4)

    @pl.when(i < n_prompt_tiles)
    def _():
        combine_into(yp_ref)

    @pl.when(i >= n_prompt_tiles)
    def _():
        combine_into(ys_ref)


def _combine(dest_flat, y_rows, probs, h1, nw, n_prompt, n_sample):
    n, d = h1.shape
    npt = n_prompt // ROW_TILE
    grid_spec = pltpu.PrefetchScalarGridSpec(
        num_scalar_prefetch=1,
        grid=(n // ROW_TILE,),
        in_specs=[pl.BlockSpec(memory_space=pl.ANY),
                  pl.BlockSpec((ROW_TILE, TOP_K), lambda i, *_: (i, 0)),
                  pl.BlockSpec((ROW_TILE, d), lambda i, *_: (i, 0)),
                  pl.BlockSpec((1, d), lambda i, *_: (0, 0))],
        out_specs=[pl.BlockSpec((ROW_TILE, d), lambda i, *_: (jnp.minimum(i, npt - 1), 0)),
                   pl.BlockSpec((ROW_TILE, d), lambda i, *_: (jnp.maximum(i - npt, 0), 0))],
        scratch_shapes=[pltpu.VMEM((2, TOP_K, ROW_TILE, d), F32), pltpu.SemaphoreType.DMA((2,))],
    )
    return pl.pallas_call(
        functools.partial(_combine_kernel, n_prompt_tiles=npt),
        grid_spec=grid_spec,
        out_shape=[jax.ShapeDtypeStruct((n_prompt, d), F32), jax.ShapeDtypeStruct((n_sample, d), F32)],
        compiler_params=_params(("arbitrary",)),
        name="moe_combine",
    )(dest_flat, y_rows, probs, h1, nw)


def _row_tile(n, cap, mult):
    return max(t for t in range(mult, min(n, cap) + 1, mult) if n % t == 0)


def _rope_tables(pos0, t, dk):
    inv = 1.0 / (ROPE_BASE ** jnp.linspace(0.0, 1.0, dk // 2, dtype=F32))
    ang = (pos0 + jnp.arange(t, dtype=F32))[:, None] * inv[None, :]
    cos = jnp.repeat(jnp.cos(ang), 2, axis=-1)
    sin = jnp.repeat(jnp.sin(ang), 2, axis=-1)
    sign = jnp.where(jnp.arange(dk) % 2 == 0, -1.0, 1.0).astype(F32)
    return cos, sin * sign


def kernel(x_prompt, x_sample, state_gla, state_ret, norm_mix, w_in, gla_w_gk, gla_b_gk, gla_norm_w, w_out,
           norm_ffn, router_w, router_b, w_gate, b_gate, w_up, b_up, w_down, b_down, norm_final):
    bp, tp, d = x_prompt.shape
    bs, ts, _ = x_sample.shape
    assert ts == 1 and w_in.shape[0] == 1
    dk, dv = state_gla.shape[-2], state_gla.shape[-1]
    qk, vv = HEADS * dk, HEADS * dv
    n_p, n_s = bp * tp, bs * ts
    n = n_p + n_s
    ne = router_w.shape[-1]
    assert n_p % ROW_TILE == 0 and n_s % ROW_TILE == 0

    c_rank = 2 * qk + 2 * vv
    w_gk = jnp.pad(gla_w_gk[0], ((0, LANE - GLA_RANK), (0, 0))).astype(BF16)
    b_gk = gla_b_gk[0][None, :]
    cols = (0, qk, 2 * qk, 2 * qk + vv, c_rank, c_rank + qk, c_rank + 2 * qk, c_rank + 2 * qk + vv)
    nmix = norm_mix[0][None, :]
    wo = w_out[0].astype(BF16)
    rwt = router_w[0].T.astype(BF16)
    rb = router_b[0][:, None]
    nffn = norm_ffn[0][None, :]
    gnw = gla_norm_w[0][None, :]

    xp = x_prompt.reshape(n_p, d)
    xs = x_sample.reshape(n_s, d)
    h_lg = _norm_gate(xp, nmix, w_in, c_rank, w_gk, b_gk, _row_tile(n_p, 512, 16), n, 0)
    h, lg = _norm_gate(xs, nmix, w_in, c_rank, w_gk, b_gk, n_s, n, n_p // n_s, prev=h_lg)
    proj = _in_proj(h, w_in, c_rank, tm=_row_tile(n, 1088, 16), tn=512)

    cos_p, sin_p = _rope_tables(0.0, tp, dk)
    cos_s, sin_s = _rope_tables(float(PAST_LEN), 1, dk)
    og_p, st_gla_p, og_s, or_s, s_gla_s, s_ret_s = _gla_step(
        proj, lg, gnw, cos_s, sin_s, state_gla[0], state_ret[0], bp, tp, n_p // n_s, cols)
    or_p, s_ret_p = _ret_prompt(proj, cos_p, sin_p, bp, tp, dk, dv, cols[4], cols[5], cols[6], cols[7])
    s_gla_p = jnp.swapaxes(st_gla_p, -1, -2)

    tm_p = min(256, n_p)
    h1, lt_p = _out_proj(og_p, or_p, xp, wo, nffn, rwt, rb, tm_p, n, 0)
    h1, lt_s = _out_proj(og_s, or_s, xs, wo, nffn, rwt, rb, n_s, n, n_p // n_s, h1_prev=h1)
    logits_t = jnp.concatenate([lt_p, lt_s], axis=1)

    n_slots = (n * TOP_K) // SLOT_ROWS + ne
    n_tiles_max = (n * TOP_K) // MM_ROWS + ne
    dest, probs_t, item_meta, tile_meta = _route(logits_t, n_tiles_max)
    pad_lo = item_meta[4, :ne]
    pad_hi = item_meta[5, :ne]
    dest_flat = dest.reshape(-1)
    experts = item_meta[0, :n_slots]
    nsub = item_meta[1, :n_slots]
    counts = item_meta[2:4, 0]
    tile_ids = tile_meta[0, :n_tiles_max]

    xs_rows = _gather(dest_flat, tile_ids, counts, pad_lo, pad_hi, h1, nffn, n_slots, n_tiles_max)
    hid = _moe_up(experts, nsub, counts, xs_rows.reshape(n_slots, SLOT_ROWS, d), w_gate[0], w_up[0],
                  b_gate[0][:, None, :], b_up[0][:, None, :], tn=256)
    y_rows = _moe_down(experts, nsub, counts, hid, w_down[0], b_down[0][:, None, :], tn=512)

    y_p, y_s = _combine(dest_flat, y_rows.reshape(n_slots * SLOT_ROWS, d), probs_t.T, h1,
                        norm_final[None, :], n_p, n_s)
    return (y_p.reshape(bp, tp, d), y_s.reshape(bs, ts, d),
            s_gla_p[None], s_ret_p[None], s_gla_s[None], s_ret_s[None])
```

```python
import functools
import math

import numpy as np
import jax
import jax.numpy as jnp
from jax import lax
from jax.experimental import pallas as pl
from jax.experimental.pallas import tpu as pltpu

F32 = jnp.float32
BF16 = jnp.bfloat16
I32 = jnp.int32

HEADS = 4
GLA_RANK = 16
GLA_GATE_NORM = 16.0
ROPE_BASE = 10000.0
TOP_K = 4
SWIGLU_LIMIT = 7.0
SWIGLU_ALPHA = 1.702
EPS = 1e-6
PAST_LEN = 16384

LANE = 128
ROW_TILE = 128
SLOT_TILES = 10
SLOT_ROWS = SLOT_TILES * ROW_TILE
MM_ROWS = 2 * ROW_TILE
SLOT_CHUNKS = SLOT_ROWS // MM_ROWS
GLA_CHUNK = 64
GLA_SUB = 16
RET_CHUNK = 256
VMEM_LIMIT = 60 * 1024 * 1024

_NT = (((1,), (1,)), ((), ()))
_TN = (((0,), (0,)), ((), ()))


def _params(sem, vmem=VMEM_LIMIT):
    return pltpu.CompilerParams(dimension_semantics=sem, vmem_limit_bytes=vmem)


def _sigmoid(x):
    return 1.0 / (1.0 + jnp.exp(-x))


def _rms_scale(x):
    return lax.rsqrt(jnp.mean(x * x, axis=-1, keepdims=True) + EPS)


def _norm_gate_kernel(x_ref, nw_ref, wr_ref, wgk_ref, bgk_ref, *rest):
    h_ref, lg_ref = rest[-2], rest[-1]
    x = x_ref[...]
    hb = (x * _rms_scale(x) * nw_ref[...]).astype(BF16)
    h_ref[...] = hb
    gr = lax.dot_general(hb, wr_ref[...].astype(BF16), _NT, preferred_element_type=F32)
    z = jnp.dot(gr.astype(BF16), wgk_ref[...], preferred_element_type=F32) + bgk_ref[...]
    log_sig = jnp.minimum(z, 0.0) - jnp.log1p(jnp.exp(-jnp.abs(z)))
    lg_ref[...] = log_sig * (1.0 / GLA_GATE_NORM)


def _norm_gate(x, nw, w_t, rank_col, wgk, bgk, tm, n_total, row_block0, prev=None):
    m, d = x.shape
    assert m % tm == 0 and rank_col % LANE == 0
    nlg = wgk.shape[1]
    in_specs = [
        pl.BlockSpec((tm, d), lambda i: (i, 0)),
        pl.BlockSpec((1, d), lambda i: (0, 0)),
        pl.BlockSpec((None, LANE, d), lambda i: (0, rank_col // LANE, 0)),
        pl.BlockSpec((LANE, nlg), lambda i: (0, 0)),
        pl.BlockSpec((1, nlg), lambda i: (0, 0)),
    ]
    args = [x, nw, w_t, wgk, bgk]
    aliases = {}
    if prev is not None:
        in_specs += [pl.BlockSpec(memory_space=pl.ANY)] * 2
        args += list(prev)
        aliases = {len(args) - 2: 0, len(args) - 1: 1}
    return pl.pallas_call(
        _norm_gate_kernel,
        grid=(m // tm,),
        in_specs=in_specs,
        out_specs=[pl.BlockSpec((tm, d), lambda i: (row_block0 + i, 0)),
                   pl.BlockSpec((tm, nlg), lambda i: (row_block0 + i, 0))],
        out_shape=[jax.ShapeDtypeStruct((n_total, d), BF16), jax.ShapeDtypeStruct((n_total, nlg), F32)],
        input_output_aliases=aliases,
        compiler_params=_params(("parallel",)),
        name="norm_gate",
    )(*args)


def _in_proj_kernel(h_ref, wm_ref, wx_ref, proj_ref, w_scr, *, first_shifted):
    j = pl.program_id(0)
    tn = wm_ref.shape[0]

    first = pl.program_id(1) == 0

    def project():
        proj_ref[...] = lax.dot_general(h_ref[...], w_scr[...], _NT, preferred_element_type=F32)

    @pl.when(jnp.logical_and(first, j < first_shifted))
    def _():
        w_scr[...] = wm_ref[...].astype(BF16)
        project()

    @pl.when(jnp.logical_and(first, j >= first_shifted))
    def _():
        w_scr[:tn - GLA_RANK, :] = wm_ref[GLA_RANK:, :].astype(BF16)
        w_scr[tn - GLA_RANK:, :] = wx_ref[...].astype(BF16)
        project()

    @pl.when(jnp.logical_not(first))
    def _():
        project()


def _in_proj(h, w_t, rank_col, tm, tn):
    m, d = h.shape
    n = w_t.shape[1] - GLA_RANK
    assert m % tm == 0 and n % tn == 0 and rank_col % tn == 0 and tn % GLA_RANK == 0
    per = tn // GLA_RANK
    return pl.pallas_call(
        functools.partial(_in_proj_kernel, first_shifted=rank_col // tn),
        grid=(n // tn, m // tm),
        in_specs=[
            pl.BlockSpec((tm, d), lambda j, i: (i, 0)),
            pl.BlockSpec((None, tn, d), lambda j, i: (0, j, 0)),
            pl.BlockSpec((None, GLA_RANK, d), lambda j, i: (0, (j + 1) * per, 0)),
        ],
        out_specs=pl.BlockSpec((tm, tn), lambda j, i: (i, j)),
        out_shape=jax.ShapeDtypeStruct((m, n), F32),
        scratch_shapes=[pltpu.VMEM((tn, d), BF16)],
        compiler_params=_params(("arbitrary", "arbitrary")),
        name="in_proj",
    )(h, w_t, w_t)


def _rope(x, cos, sin_signed, even):
    n = x.shape[-1]
    nxt = pltpu.roll(x, n - 1, axis=x.ndim - 1)
    prv = pltpu.roll(x, 1, axis=x.ndim - 1)
    return x * cos + jnp.where(even, nxt, prv) * sin_signed


def _ret_prompt_kernel(q_ref, k_ref, v_ref, g_ref, cos_ref, sin_ref, o_ref, s_ref):
    c = q_ref.shape[0]
    dk = q_ref.shape[1] // HEADS
    dv = v_ref.shape[1] // HEADS

    @pl.when(pl.program_id(1) == 0)
    def _():
        s_ref[...] = jnp.zeros_like(s_ref)

    ti = lax.broadcasted_iota(I32, (c, c), 0)
    si = lax.broadcasted_iota(I32, (c, c), 1)
    dlt = (ti - si).astype(F32)
    causal = ti >= si
    tcol = lax.broadcasted_iota(I32, (c, 1), 0).astype(F32)
    even = (lax.broadcasted_iota(I32, (c, dk), 1) % 2) == 0
    cos = cos_ref[...]
    sin = sin_ref[...]
    for h in range(HEADS):
        lgam = math.log(1.0 - 2.0 ** (-5.0 - h))
        q = _rope(q_ref[:, h * dk:(h + 1) * dk], cos, sin, even)
        k = _rope(k_ref[:, h * dk:(h + 1) * dk], cos, sin, even) * (dk ** -0.5)
        qb = q.astype(BF16)
        vb = v_ref[:, h * dv:(h + 1) * dv].astype(BF16)
        decay = jnp.where(causal, jnp.exp(dlt * lgam), 0.0)
        a = lax.dot_general(qb, k.astype(BF16), _NT, preferred_element_type=F32) * decay
        s = s_ref[0, h]
        o = jnp.dot(a.astype(BF16), vb, preferred_element_type=F32)
        o = o + jnp.dot(qb, s.astype(BF16), preferred_element_type=F32) * jnp.exp((tcol + 1.0) * lgam)
        kd = (k * jnp.exp((c - 1.0 - tcol) * lgam)).astype(BF16)
        s_ref[0, h] = math.exp(c * lgam) * s + lax.dot_general(kd, vb, _TN, preferred_element_type=F32)
        g = g_ref[:, h * dv:(h + 1) * dv]
        o_ref[:, h * dv:(h + 1) * dv] = (o * _rms_scale(o) * (g * _sigmoid(g))).astype(BF16)


def _ret_prompt(proj, cos, sin, batch, seq, dk, dv, col_q, col_k, col_v, col_g):
    c = math.gcd(seq, RET_CHUNK)
    nt = seq // c
    qw, vw = HEADS * dk, HEADS * dv
    return pl.pallas_call(
        _ret_prompt_kernel,
        grid=(batch, nt),
        in_specs=[
            pl.BlockSpec((c, qw), lambda b, t: (b * nt + t, col_q // qw)),
            pl.BlockSpec((c, qw), lambda b, t: (b * nt + t, col_k // qw)),
            pl.BlockSpec((c, vw), lambda b, t: (b * nt + t, col_v // vw)),
            pl.BlockSpec((c, vw), lambda b, t: (b * nt + t, col_g // vw)),
            pl.BlockSpec((c, dk), lambda b, t: (t, 0)),
            pl.BlockSpec((c, dk), lambda b, t: (t, 0)),
        ],
        out_specs=[
            pl.BlockSpec((c, vw), lambda b, t: (b * nt + t, 0)),
            pl.BlockSpec((1, HEADS, dk, dv), lambda b, t: (b, 0, 0, 0)),
        ],
        out_shape=[jax.ShapeDtypeStruct((batch * seq, vw), BF16),
                   jax.ShapeDtypeStruct((batch, HEADS, dk, dv), F32)],
        compiler_params=_params(("parallel", "arbitrary")),
        name="ret_prompt",
    )(proj, proj, proj, proj, cos, sin)


def _gla_chunk(q_ref, k_ref, v_ref, g_ref, lg_ref, nw_ref, o_ref, st_ref, b_scr, a_scr):
    c = q_ref.shape[0]
    dk = q_ref.shape[1] // HEADS
    dv = v_ref.shape[1] // HEADS
    sub = GLA_SUB
    nsub = c // sub

    @pl.when(pl.program_id(1) == 0)
    def _():
        st_ref[...] = jnp.zeros_like(st_ref)

    ri = lax.broadcasted_iota(I32, (c, c), 0)
    ci = lax.broadcasted_iota(I32, (c, c), 1)
    lower = (ri >= ci).astype(BF16)
    cis = lax.broadcasted_iota(I32, (sub, c), 1)

    for h in range(HEADS):
        lg = lg_ref[:, h * dk:(h + 1) * dk]
        l1 = lg.astype(BF16)
        r1 = lg - l1.astype(F32)
        l2 = r1.astype(BF16)
        l3 = (r1 - l2.astype(F32)).astype(BF16)
        b_scr[h] = (jnp.dot(lower, l1, preferred_element_type=F32)
                    + jnp.dot(lower, l2, preferred_element_type=F32)
                    + jnp.dot(lower, l3, preferred_element_type=F32))

    def sub_block(i, carry):
        r0 = pl.multiple_of(i * sub, sub)
        rprev = jnp.maximum(r0 - 1, 0)
        for h in range(HEADS):
            hs = slice(h * dk, (h + 1) * dk)
            qi = q_ref[pl.ds(r0, sub), hs] * (dk ** -0.5)
            ki = k_ref[pl.ds(r0, sub), hs]
            bi = b_scr[h, pl.ds(r0, sub), :]
            ad = jnp.zeros((sub, c), F32)
            for s in range(sub):
                e = jnp.exp(jnp.minimum(bi - bi[s:s + 1, :], 0.0))
                col = jnp.sum(qi * e * ki[s:s + 1, :], axis=1, keepdims=True)
                ad = jnp.where(cis == r0 + s, col, ad)
            bref = b_scr[h, pl.ds(rprev, 1), :]
            qq = (qi * jnp.exp(jnp.minimum(bi - bref, 0.0))).astype(BF16)
            kk = (k_ref[:, hs] * jnp.exp(jnp.minimum(bref - b_scr[h], 0.0))).astype(BF16)
            ao = lax.dot_general(qq, kk, _NT, preferred_element_type=F32)
            a_scr[h, pl.ds(r0, sub), :] = jnp.where(cis < r0, ao, ad)
        return carry

    lax.fori_loop(0, nsub, sub_block, 0)

    for h in range(HEADS):
        hs = slice(h * dk, (h + 1) * dk)
        vs = slice(h * dv, (h + 1) * dv)
        b = b_scr[h]
        b_last = b[c - 1:c, :]
        vb = v_ref[:, vs].astype(BF16)
        qe = (q_ref[:, hs] * (dk ** -0.5) * jnp.exp(b)).astype(BF16)
        kd = (k_ref[:, hs] * jnp.exp(b_last - b)).astype(BF16)
        st = st_ref[0, h]
        a = jnp.where(ri >= ci, a_scr[h], 0.0)
        o = jnp.dot(a.astype(BF16), vb, preferred_element_type=F32)
        o = o + lax.dot_general(qe, st.astype(BF16), _NT, preferred_element_type=F32)
        st_ref[0, h] = st * jnp.exp(b_last) + lax.dot_general(vb, kd, _TN, preferred_element_type=F32)
        g = g_ref[:, vs]
        o_ref[:, vs] = (o * _rms_scale(o) * nw_ref[...] * (g * _sigmoid(g))).astype(BF16)


def _to_col(row, eye):
    return jnp.sum(jnp.where(eye, row, 0.0), axis=1, keepdims=True)


def _step_rows(b, seq, proj_ref, lg_ref, cos_ref, sin_ref, nw_ref, sg_ref, sr_ref,
               og_ref, or_ref, ng_ref, nr_ref, cols):
    dk, dv = sg_ref.shape[-2], sg_ref.shape[-1]
    col_gq, col_gk, col_gv, col_gg, col_rq, col_rk, col_rv, col_rg = cols
    eye = lax.broadcasted_iota(I32, (dk, dk), 0) == lax.broadcasted_iota(I32, (dk, dk), 1)
    even = (lax.broadcasted_iota(I32, (1, dk), 1) % 2) == 0
    row = pl.ds(b, 1)

    def finish(o, g, w):
        o = o * _rms_scale(o)
        if w is not None:
            o = o * w
        return o * (g * _sigmoid(g))

    for h in range(HEADS):
        q = proj_ref[row, col_gq + h * dk:col_gq + (h + 1) * dk] * (dk ** -0.5)
        k = proj_ref[row, col_gk + h * dk:col_gk + (h + 1) * dk]
        v = proj_ref[row, col_gv + h * dv:col_gv + (h + 1) * dv]
        g = proj_ref[row, col_gg + h * dv:col_gg + (h + 1) * dv]
        a = jnp.exp(lg_ref[row, h * dk:(h + 1) * dk])
        s_new = _to_col(a, eye) * sg_ref[seq, h] + _to_col(k, eye) * v
        ng_ref[seq, h] = s_new
        o = jnp.sum(_to_col(q, eye) * s_new, axis=0, keepdims=True)
        og_ref[row, h * dv:(h + 1) * dv] = finish(o, g, nw_ref[...])

        gamma = 1.0 - 2.0 ** (-5.0 - h)
        q = _rope(proj_ref[row, col_rq + h * dk:col_rq + (h + 1) * dk], cos_ref[...], sin_ref[...], even)
        k = _rope(proj_ref[row, col_rk + h * dk:col_rk + (h + 1) * dk], cos_ref[...], sin_ref[...], even)
        k = k * (dk ** -0.5)
        v = proj_ref[row, col_rv + h * dv:col_rv + (h + 1) * dv]
        g = proj_ref[row, col_rg + h * dv:col_rg + (h + 1) * dv]
        s_new = gamma * sr_ref[seq, h] + _to_col(k, eye) * v
        nr_ref[seq, h] = s_new
        o = jnp.sum(_to_col(q, eye) * s_new, axis=0, keepdims=True)
        or_ref[row, h * dv:(h + 1) * dv] = finish(o, g, None)


def _gla_step_kernel(q_ref, k_ref, v_ref, g_ref, lg_ref, nw_ref, sproj_ref, slg_ref, cos_ref, sin_ref,
                     sg_ref, sr_ref, o_ref, st_ref, og_ref, or_ref, ng_ref, nr_ref, b_scr, a_scr,
                     *, nt, cols):
    _gla_chunk(q_ref, k_ref, v_ref, g_ref, lg_ref, nw_ref, o_ref, st_ref, b_scr, a_scr)
    per_step = sg_ref.shape[0]
    first = (pl.program_id(0) * nt + pl.program_id(1)) * per_step
    for s in range(per_step):
        _step_rows(first + s, s, sproj_ref, slg_ref, cos_ref, sin_ref, nw_ref, sg_ref, sr_ref,
                   og_ref, or_ref, ng_ref, nr_ref, cols)


def _gla_step(proj, lg, nw, cos_s, sin_s, s_gla, s_ret, batch, seq, sample_row_block, cols):
    nb, _, dk, dv = s_gla.shape
    c = math.gcd(seq, GLA_CHUNK)
    nt = seq // c
    assert nb % (batch * nt) == 0
    per_step = nb // (batch * nt)
    qw, vw = HEADS * dk, HEADS * dv
    ncol = proj.shape[1]
    col_q, col_k, col_v, col_g = cols[:4]
    blk = lambda width, col: pl.BlockSpec((c, width), lambda b, t: (b * nt + t, col // width))
    full = lambda shape: pl.BlockSpec(shape, lambda b, t: (0,) * len(shape))
    rows = lambda width: pl.BlockSpec((nb, width), lambda b, t: (sample_row_block, 0))
    state = pl.BlockSpec((per_step, HEADS, dk, dv), lambda b, t: (b * nt + t, 0, 0, 0))
    return pl.pallas_call(
        functools.partial(_gla_step_kernel, nt=nt, cols=cols),
        grid=(batch, nt),
        in_specs=[blk(qw, col_q), blk(qw, col_k), blk(vw, col_v), blk(vw, col_g), blk(qw, 0), full((1, dv)),
                  rows(ncol), rows(qw), full((1, dk)), full((1, dk)), state, state],
        out_specs=[
            pl.BlockSpec((c, vw), lambda b, t: (b * nt + t, 0)),
            pl.BlockSpec((1, HEADS, dv, dk), lambda b, t: (b, 0, 0, 0)),
            full((nb, vw)), full((nb, vw)), state, state,
        ],
        out_shape=[jax.ShapeDtypeStruct((batch * seq, vw), BF16),
                   jax.ShapeDtypeStruct((batch, HEADS, dv, dk), F32),
                   jax.ShapeDtypeStruct((nb, vw), F32), jax.ShapeDtypeStruct((nb, vw), F32),
                   jax.ShapeDtypeStruct(s_gla.shape, F32), jax.ShapeDtypeStruct(s_ret.shape, F32)],
        scratch_shapes=[pltpu.VMEM((HEADS, c, dk), F32), pltpu.VMEM((HEADS, c, c), F32)],
        compiler_params=_params(("arbitrary", "arbitrary")),
        name="gla_step",
    )(proj, proj, proj, proj, lg, nw, proj, lg, cos_s, sin_s, s_gla, s_ret)


def _out_proj_kernel(og_ref, or_ref, x_ref, wg_ref, wr_ref, nw_ref, rwt_ref, rb_ref, *rest):
    h1_ref, lt_ref = rest[-2], rest[-1]
    acc = jnp.dot(og_ref[...].astype(BF16), wg_ref[...], preferred_element_type=F32)
    acc = acc + jnp.dot(or_ref[...].astype(BF16), wr_ref[...], preferred_element_type=F32)
    h1 = x_ref[...] + acc
    h1_ref[...] = h1
    flat = (h1 * _rms_scale(h1) * nw_ref[...]).astype(BF16)
    lt_ref[...] = lax.dot_general(rwt_ref[...], flat, _NT, preferred_element_type=F32) + rb_ref[...]


def _out_proj(og, orr, x, wo, nw, rwt, rb, tm, n_total, row_block0, h1_prev=None):
    m, d = x.shape
    half = og.shape[1]
    ne = rwt.shape[0]
    const = lambda shape: pl.BlockSpec(shape, lambda i: (0,) * len(shape), pipeline_mode=pl.Buffered(1))
    in_specs = [
        pl.BlockSpec((tm, half), lambda i: (i, 0)),
        pl.BlockSpec((tm, half), lambda i: (i, 0)),
        pl.BlockSpec((tm, d), lambda i: (i, 0)),
        const((half, d)), pl.BlockSpec((half, d), lambda i: (1, 0), pipeline_mode=pl.Buffered(1)),
        const((1, d)), const((ne, d)), const((ne, 1)),
    ]
    args = [og, orr, x, wo, wo, nw, rwt, rb]
    aliases = {}
    if h1_prev is not None:
        in_specs.append(pl.BlockSpec(memory_space=pl.ANY))
        args.append(h1_prev)
        aliases = {len(args) - 1: 0}
    return pl.pallas_call(
        _out_proj_kernel,
        grid=(m // tm,),
        in_specs=in_specs,
        out_specs=[
            pl.BlockSpec((tm, d), lambda i: (row_block0 + i, 0)),
            pl.BlockSpec((ne, tm), lambda i: (0, i)),
        ],
        out_shape=[jax.ShapeDtypeStruct((n_total, d), F32), jax.ShapeDtypeStruct((ne, m), F32)],
        input_output_aliases=aliases,
        compiler_params=_params(("parallel",)),
        name="out_proj",
    )(*args)


def _route_kernel(lt_ref, dest_ref, prob_ref, item_ref, tile_ref, m_scr, pos_scr):
    ne, n = lt_ref.shape
    nblk = n // LANE
    logits = lt_ref[...]
    eio = lax.broadcasted_iota(I32, (ne, n), 0).astype(F32)
    vals, idxs = [], []
    for _ in range(TOP_K):
        m = jnp.max(logits, axis=0, keepdims=True)
        ik = jnp.min(jnp.where(logits == m, eio, float(ne)), axis=0, keepdims=True)
        vals.append(m)
        idxs.append(ik)
        logits = jnp.where(eio == ik, -jnp.inf, logits)
    ex = [jnp.exp(v - vals[0]) for v in vals]
    den = ex[0] + ex[1] + ex[2] + ex[3]
    for k in range(TOP_K):
        prob_ref[k:k + 1, :] = ex[k] / den
    sel = jnp.zeros((ne, n), F32)
    for ik in idxs:
        sel = jnp.where(eio == ik, 1.0, sel)
    m_scr[...] = sel

    upper = (lax.broadcasted_iota(I32, (LANE, LANE), 0) <= lax.broadcasted_iota(I32, (LANE, LANE), 1)).astype(BF16)

    def prefix(cb, carry):
        c0 = pl.multiple_of(cb * LANE, LANE)
        mc = m_scr[:, pl.ds(c0, LANE)]
        incl = jnp.dot(mc.astype(BF16), upper, preferred_element_type=F32)
        pos_scr[:, pl.ds(c0, LANE)] = incl - mc + carry
        return carry + jnp.sum(mc, axis=1, keepdims=True)

    cnt = lax.fori_loop(0, nblk, prefix, jnp.zeros((ne, 1), F32))

    def ceil_div(x, step, most):
        r = jnp.zeros_like(x)
        for mlt in range(most):
            r = r + jnp.where(x > float(mlt * step), 1.0, 0.0)
        return r

    cnt_l = jnp.broadcast_to(cnt, (ne, LANE))
    n_slot = ceil_div(cnt_l, SLOT_ROWS, -(-n // SLOT_ROWS))
    n_tile = ceil_div(cnt_l, MM_ROWS, -(-n // MM_ROWS))
    strict = (lax.broadcasted_iota(I32, (ne, ne), 0) > lax.broadcasted_iota(I32, (ne, ne), 1)).astype(BF16)
    slot0 = jnp.dot(strict, n_slot.astype(BF16), preferred_element_type=F32)
    tile0 = jnp.dot(strict, n_tile.astype(BF16), preferred_element_type=F32)

    destf = slot0[:, :1] * float(SLOT_ROWS) + pos_scr[...]
    for k in range(TOP_K):
        dk_ = jnp.sum(jnp.where(eio == idxs[k], destf, 0.0), axis=0, keepdims=True)
        dest_ref[k:k + 1, :] = dk_.astype(I32)

    def lookup(table, onehot):
        return jnp.sum(jnp.where(onehot, table, 0.0), axis=0, keepdims=True)

    e_col = lax.broadcasted_iota(I32, (ne, LANE), 0).astype(F32)
    it = lax.broadcasted_iota(I32, (ne, LANE), 1).astype(F32)
    n_items = jnp.sum(n_slot, axis=0, keepdims=True)
    e_it = jnp.minimum(jnp.sum(jnp.where(slot0 + n_slot <= it, 1.0, 0.0), axis=0, keepdims=True), ne - 1.0)
    oh = e_col == e_it
    rows = lookup(cnt_l, oh) - (it[:1] - lookup(slot0, oh)) * float(SLOT_ROWS)
    rows = jnp.clip(rows, 0.0, float(SLOT_ROWS))
    rows = jnp.where(it[:1] < n_items, rows, 0.0)
    item_ref[...] = jnp.zeros_like(item_ref)
    item_ref[0:1, :] = e_it.astype(I32)
    item_ref[1:2, :] = ceil_div(rows, ROW_TILE, SLOT_TILES).astype(I32)
    item_ref[2:3, :] = n_items.astype(I32)
    item_ref[3:4, :] = jnp.sum(n_tile, axis=0, keepdims=True).astype(I32)
    diag = e_col == it
    item_ref[4:5, :] = lookup(slot0 * float(SLOT_ROWS) + cnt_l, diag).astype(I32)
    item_ref[5:6, :] = lookup(slot0 * float(SLOT_ROWS) + n_tile * float(MM_ROWS), diag).astype(I32)

    ntl = tile_ref.shape[1]
    tile_ref[...] = jnp.zeros_like(tile_ref)
    for cb in range(ntl // LANE):
        gi = it + float(cb * LANE)
        e_g = jnp.minimum(jnp.sum(jnp.where(tile0 + n_tile <= gi, 1.0, 0.0), axis=0, keepdims=True), ne - 1.0)
        ohg = e_col == e_g
        tid = lookup(slot0, ohg) * float(SLOT_CHUNKS) + (gi[:1] - lookup(tile0, ohg))
        tile_ref[0:1, cb * LANE:(cb + 1) * LANE] = tid.astype(I32)


def _route(logits_t, n_tiles_max):
    ne, n = logits_t.shape
    ntl = -(-n_tiles_max // LANE) * LANE
    return pl.pallas_call(
        _route_kernel,
        out_shape=[jax.ShapeDtypeStruct((TOP_K, n), I32), jax.ShapeDtypeStruct((TOP_K, n), F32),
                   jax.ShapeDtypeStruct((8, LANE), I32), jax.ShapeDtypeStruct((8, ntl), I32)],
        scratch_shapes=[pltpu.VMEM((ne, n), F32), pltpu.VMEM((ne, n), F32)],
        compiler_params=pltpu.CompilerParams(vmem_limit_bytes=VMEM_LIMIT),
        name="route",
    )(logits_t)


def _gather_kernel(dest_sm, tile_sm, cnt_sm, pad_lo_sm, pad_hi_sm, h1_hbm, nw_ref, xs_ref, inv_sm, buf, sem):
    g = pl.program_id(0)
    n_tiles = cnt_sm[1]
    n_tok = dest_sm.shape[0] // TOP_K
    group = 16

    def issue(gi, slot):
        base = tile_sm[gi] * MM_ROWS

        def body(r, carry):
            src = inv_sm[base + r]
            pltpu.make_async_copy(h1_hbm.at[pl.ds(src, 1)], buf.at[slot, pl.ds(r, 1)], sem.at[slot]).start()
            return carry

        lax.fori_loop(0, MM_ROWS, body, 0, unroll=8)

    @pl.when(g == 0)
    def _():
        spread = (1 << (n_tok.bit_length() - 1)) - 1

        def pad_expert(e, carry):
            def clear(i, c2):
                inv_sm[i] = i & spread
                return c2

            lax.fori_loop(pad_lo_sm[e], pad_hi_sm[e], clear, 0)
            return carry

        lax.fori_loop(0, pad_lo_sm.shape[0], pad_expert, 0)
        for k in range(TOP_K):
            def scatter(t, carry):
                inv_sm[dest_sm[k * n_tok + t]] = t
                return carry

            lax.fori_loop(0, n_tok, scatter, 0, unroll=8)
        issue(0, 0)

    @pl.when(g + 1 < n_tiles)
    def _():
        issue(g + 1, (g + 1) % 2)

    @pl.when(g < n_tiles)
    def _():
        slot = g % 2
        pltpu.make_async_copy(h1_hbm.at[pl.ds(0, MM_ROWS)], buf.at[slot], sem.at[slot]).wait()

        def norm_rows(r, carry):
            r0 = pl.multiple_of(r * group, group)
            x = buf[slot, pl.ds(r0, group), :]
            xs_ref[pl.ds(r0, group), :] = (x * _rms_scale(x) * nw_ref[...]).astype(BF16)
            return carry

        lax.fori_loop(0, MM_ROWS // group, norm_rows, 0, unroll=4)


def _gather(dest_flat, tile_ids, counts, pad_lo, pad_hi, h1, nw, n_slots, n_tiles_max):
    d = h1.shape[1]

    def out_map(g, dest_sm, tile_sm, cnt_sm, *_):
        return (tile_sm[jnp.minimum(g, cnt_sm[1] - 1)], 0)

    grid_spec = pltpu.PrefetchScalarGridSpec(
        num_scalar_prefetch=5,
        grid=(n_tiles_max,),
        in_specs=[pl.BlockSpec(memory_space=pl.ANY),
                  pl.BlockSpec((1, d), lambda g, *_: (0, 0))],
        out_specs=pl.BlockSpec((MM_ROWS, d), out_map),
        scratch_shapes=[pltpu.SMEM((n_slots * SLOT_ROWS,), I32),
                        pltpu.VMEM((2, MM_ROWS, d), F32),
                        pltpu.SemaphoreType.DMA((2,))],
    )
    return pl.pallas_call(
        _gather_kernel,
        grid_spec=grid_spec,
        out_shape=jax.ShapeDtypeStruct((n_slots * SLOT_ROWS, d), BF16),
        compiler_params=_params(("arbitrary",)),
        name="moe_gather",
    )(dest_flat, tile_ids, counts, pad_lo, pad_hi, h1, nw)


def _for_row_chunks(ns, matmul, finish, clear, acc_scr):
    acc_a, acc_b = acc_scr
    odd = jnp.logical_and(ns > 1, (ns & 1) == 1)
    n_full = jnp.maximum(lax.shift_right_logical(ns, 1), 1)
    n_pair = lax.shift_right_logical(n_full - 1, 1)
    row0 = lambda c: pl.multiple_of(c * MM_ROWS, MM_ROWS)
    acc_a[...] = matmul(0, MM_ROWS)

    def pair(p, carry):
        c = 2 * p
        acc_b[...] = matmul(row0(c + 1), MM_ROWS)
        finish(row0(c), MM_ROWS, acc_a[...])
        acc_a[...] = matmul(row0(c + 2), MM_ROWS)
        finish(row0(c + 1), MM_ROWS, acc_b[...])
        return carry

    lax.fori_loop(0, n_pair, pair, 0)
    done = 2 * n_pair
    extra = (n_full - 1) > done
    t0 = pl.multiple_of((ns - 1) * ROW_TILE, ROW_TILE)

    @pl.when(extra)
    def _():
        acc_b[...] = matmul(row0(done + 1), MM_ROWS)
        finish(row0(done), MM_ROWS, acc_a[...])

    def drain(acc, c):
        @pl.when(odd)
        def _():
            tail = matmul(t0, ROW_TILE)
            finish(row0(c), MM_ROWS, acc[...])
            finish(t0, ROW_TILE, tail)

        @pl.when(jnp.logical_not(odd))
        def _():
            finish(row0(c), MM_ROWS, acc[...])

    @pl.when(extra)
    def _():
        drain(acc_b, done + 1)

    @pl.when(jnp.logical_not(extra))
    def _():
        drain(acc_a, done)

    def clear_tile(s, carry):
        clear(pl.multiple_of(s * ROW_TILE, ROW_TILE))
        return carry

    lax.fori_loop(jnp.maximum(ns, MM_ROWS // ROW_TILE), SLOT_TILES, clear_tile, 0)


def _moe_up_kernel(exp_sm, nsub_sm, cnt_sm, x_ref, wg_ref, wu_ref, bg_ref, bu_ref, hid_ref, w_scr, *acc_scr):
    it = pl.program_id(0)
    tn = wg_ref.shape[2]

    @pl.when(it < cnt_sm[0])
    def _():
        w_scr[:, :tn] = wg_ref[0].astype(BF16)
        w_scr[:, tn:] = wu_ref[0].astype(BF16)

        def matmul(r0, rows):
            return jnp.dot(x_ref[0, pl.ds(r0, rows), :], w_scr[...], preferred_element_type=F32)

        def finish(r0, rows, gu):
            g = jnp.minimum(gu[:, :tn] + bg_ref[0], SWIGLU_LIMIT)
            u = jnp.clip(gu[:, tn:] + bu_ref[0], -SWIGLU_LIMIT, SWIGLU_LIMIT)
            hid = (u + 1.0) * g * _sigmoid(SWIGLU_ALPHA * g)
            hid_ref[0, pl.ds(r0, rows), :] = hid.astype(BF16)

        def clear(r0):
            hid_ref[0, pl.ds(r0, ROW_TILE), :] = jnp.zeros((ROW_TILE, tn), BF16)

        _for_row_chunks(nsub_sm[it], matmul, finish, clear, acc_scr)


def _moe_down_kernel(exp_sm, nsub_sm, cnt_sm, h_ref, w_ref, b_ref, y_ref, w_scr, *acc_scr):
    it = pl.program_id(0)
    tn = w_ref.shape[2]

    @pl.when(it < cnt_sm[0])
    def _():
        w_scr[...] = w_ref[0].astype(BF16)

        def matmul(r0, rows):
            return jnp.dot(h_ref[0, pl.ds(r0, rows), :], w_scr[...], preferred_element_type=F32)

        def finish(r0, rows, y):
            y_ref[0, pl.ds(r0, rows), :] = y + b_ref[0]

        def clear(r0):
            y_ref[0, pl.ds(r0, ROW_TILE), :] = jnp.zeros((ROW_TILE, tn), F32)

        _for_row_chunks(nsub_sm[it], matmul, finish, clear, acc_scr)


def _item_maps(nj):
    def eff(it, j, cnt_sm):
        last = cnt_sm[0] - 1
        return jnp.minimum(it, last), jnp.where(it <= last, j, nj - 1)

    def x_map(it, j, exp_sm, nsub_sm, cnt_sm):
        ie, _ = eff(it, j, cnt_sm)
        return (ie, 0, 0)

    def w_map(it, j, exp_sm, nsub_sm, cnt_sm):
        ie, je = eff(it, j, cnt_sm)
        return (exp_sm[ie], 0, je)

    def o_map(it, j, exp_sm, nsub_sm, cnt_sm):
        ie, je = eff(it, j, cnt_sm)
        return (ie, 0, je)

    return x_map, w_map, o_map


def _moe_up(experts, nsub, counts, xs, w_gate, w_up, b_gate, b_up, tn):
    n_slots, rows, d = xs.shape
    dff = w_gate.shape[2]
    nj = dff // tn
    x_map, w_map, o_map = _item_maps(nj)
    grid_spec = pltpu.PrefetchScalarGridSpec(
        num_scalar_prefetch=3,
        grid=(counts[0], nj),
        in_specs=[pl.BlockSpec((1, rows, d), x_map),
                  pl.BlockSpec((1, d, tn), w_map), pl.BlockSpec((1, d, tn), w_map),
                  pl.BlockSpec((1, 1, tn), w_map), pl.BlockSpec((1, 1, tn), w_map)],
        out_specs=pl.BlockSpec((1, rows, tn), o_map),
        scratch_shapes=[pltpu.VMEM((d, 2 * tn), BF16),
                        pltpu.VMEM((MM_ROWS, 2 * tn), F32), pltpu.VMEM((MM_ROWS, 2 * tn), F32)],
    )
    return pl.pallas_call(
        _moe_up_kernel,
        grid_spec=grid_spec,
        out_shape=jax.ShapeDtypeStruct((n_slots, rows, dff), BF16),
        compiler_params=_params(("arbitrary", "arbitrary")),
        name="moe_up",
    )(experts, nsub, counts, xs, w_gate, w_up, b_gate, b_up)


def _moe_down(experts, nsub, counts, hid, w_down, b_down, tn):
    n_slots, rows, dff = hid.shape
    d = w_down.shape[2]
    nj = d // tn
    x_map, w_map, o_map = _item_maps(nj)
    grid_spec = pltpu.PrefetchScalarGridSpec(
        num_scalar_prefetch=3,
        grid=(counts[0], nj),
        in_specs=[pl.BlockSpec((1, rows, dff), x_map),
                  pl.BlockSpec((1, dff, tn), w_map),
                  pl.BlockSpec((1, 1, tn), w_map)],
        out_specs=pl.BlockSpec((1, rows, tn), o_map),
        scratch_shapes=[pltpu.VMEM((dff, tn), BF16),
                        pltpu.VMEM((MM_ROWS, tn), F32), pltpu.VMEM((MM_ROWS, tn), F32)],
    )
    return pl.pallas_call(
        _moe_down_kernel,
        grid_spec=grid_spec,
        out_shape=jax.ShapeDtypeStruct((n_slots, rows, d), F32),
        compiler_params=_params(("arbitrary", "arbitrary")),
        name="moe_down",
    )(experts, nsub, counts, hid, w_down, b_down)


def _combine_kernel(dest_sm, y_hbm, p_ref, h1_ref, nw_ref, yp_ref, ys_ref, buf, sem, *, n_prompt_tiles):
    i = pl.program_id(0)
    n_steps = pl.num_programs(0)
    n_tok = dest_sm.shape[0] // TOP_K

    def issue(ti, slot):
        for k in range(TOP_K):
            def body(r, carry):
                src = dest_sm[k * n_tok + ti * ROW_TILE + r]
                pltpu.make_async_copy(y_hbm.at[pl.ds(src, 1)], buf.at[slot, k, pl.ds(r, 1)], sem.at[slot]).start()
                return carry

            lax.fori_loop(0, ROW_TILE, body, 0, unroll=8)

    @pl.when(i == 0)
    def _():
        issue(0, 0)

    @pl.when(i + 1 < n_steps)
    def _():
        issue(i + 1, (i + 1) % 2)

    slot = i % 2
    for k in range(TOP_K):
        pltpu.make_async_copy(y_hbm.at[pl.ds(0, ROW_TILE)], buf.at[slot, k], sem.at[slot]).wait()
    group = 16

    def combine_into(out_ref):
        def rows(r, carry):
            r0 = pl.multiple_of(r * group, group)
            acc = h1_ref[pl.ds(r0, group), :]
            for k in range(TOP_K):
                acc = acc + p_ref[pl.ds(r0, group), k:k + 1] * buf[slot, k, pl.ds(r0, group), :]
            out_ref[pl.ds(r0, group), :] = acc * _rms_scale(acc) * nw_ref[...]
            return carry

        lax.fori_loop(0, ROW_TILE // group, rows, 0, unroll=4)

    @pl.when(i < n_prompt_tiles)
    def _():
        combine_into(yp_ref)

    @pl.when(i >= n_prompt_tiles)
    def _():
        combine_into(ys_ref)


def _combine(dest_flat, y_rows, probs, h1, nw, n_prompt, n_sample):
    n, d = h1.shape
    npt = n_prompt // ROW_TILE
    grid_spec = pltpu.PrefetchScalarGridSpec(
        num_scalar_prefetch=1,
        grid=(n // ROW_TILE,),
        in_specs=[pl.BlockSpec(memory_space=pl.ANY),
                  pl.BlockSpec((ROW_TILE, TOP_K), lambda i, *_: (i, 0)),
                  pl.BlockSpec((ROW_TILE, d), lambda i, *_: (i, 0)),
                  pl.BlockSpec((1, d), lambda i, *_: (0, 0))],
        out_specs=[pl.BlockSpec((ROW_TILE, d), lambda i, *_: (jnp.minimum(i, npt - 1), 0)),
                   pl.BlockSpec((ROW_TILE, d), lambda i, *_: (jnp.maximum(i - npt, 0), 0))],
        scratch_shapes=[pltpu.VMEM((2, TOP_K, ROW_TILE, d), F32), pltpu.SemaphoreType.DMA((2,))],
    )
    return pl.pallas_call(
        functools.partial(_combine_kernel, n_prompt_tiles=npt),
        grid_spec=grid_spec,
        out_shape=[jax.ShapeDtypeStruct((n_prompt, d), F32), jax.ShapeDtypeStruct((n_sample, d), F32)],
        compiler_params=_params(("arbitrary",)),
        name="moe_combine",
    )(dest_flat, y_rows, probs, h1, nw)


def _row_tile(n, cap, mult):
    return max(t for t in range(mult, min(n, cap) + 1, mult) if n % t == 0)


def _rope_tables(pos0, t, dk):
    inv = 1.0 / (ROPE_BASE ** jnp.linspace(0.0, 1.0, dk // 2, dtype=F32))
    ang = (pos0 + jnp.arange(t, dtype=F32))[:, None] * inv[None, :]
    cos = jnp.repeat(jnp.cos(ang), 2, axis=-1)
    sin = jnp.repeat(jnp.sin(ang), 2, axis=-1)
    sign = jnp.where(jnp.arange(dk) % 2 == 0, -1.0, 1.0).astype(F32)
    return cos, sin * sign


def kernel(x_prompt, x_sample, state_gla, state_ret, norm_mix, w_in, gla_w_gk, gla_b_gk, gla_norm_w, w_out,
           norm_ffn, router_w, router_b, w_gate, b_gate, w_up, b_up, w_down, b_down, norm_final):
    bp, tp, d = x_prompt.shape
    bs, ts, _ = x_sample.shape
    assert ts == 1 and w_in.shape[0] == 1
    dk, dv = state_gla.shape[-2], state_gla.shape[-1]
    qk, vv = HEADS * dk, HEADS * dv
    n_p, n_s = bp * tp, bs * ts
    n = n_p + n_s
    ne = router_w.shape[-1]
    assert n_p % ROW_TILE == 0 and n_s % ROW_TILE == 0

    c_rank = 2 * qk + 2 * vv
    w_gk = jnp.pad(gla_w_gk[0], ((0, LANE - GLA_RANK), (0, 0))).astype(BF16)
    b_gk = gla_b_gk[0][None, :]
    cols = (0, qk, 2 * qk, 2 * qk + vv, c_rank, c_rank + qk, c_rank + 2 * qk, c_rank + 2 * qk + vv)
    nmix = norm_mix[0][None, :]
    wo = w_out[0].astype(BF16)
    rwt = router_w[0].T.astype(BF16)
    rb = router_b[0][:, None]
    nffn = norm_ffn[0][None, :]
    gnw = gla_norm_w[0][None, :]

    xp = x_prompt.reshape(n_p, d)
    xs = x_sample.reshape(n_s, d)
    w_t = jnp.swapaxes(w_in, 1, 2)
    h_lg = _norm_gate(xp, nmix, w_t, c_rank, w_gk, b_gk, _row_tile(n_p, 512, 16), n, 0)
    h, lg = _norm_gate(xs, nmix, w_t, c_rank, w_gk, b_gk, n_s, n, n_p // n_s, prev=h_lg)
    proj = _in_proj(h, w_t, c_rank, tm=_row_tile(n, 1088, 16), tn=512)

    cos_p, sin_p = _rope_tables(0.0, tp, dk)
    cos_s, sin_s = _rope_tables(float(PAST_LEN), 1, dk)
    og_p, st_gla_p, og_s, or_s, s_gla_s, s_ret_s = _gla_step(
        proj, lg, gnw, cos_s, sin_s, state_gla[0], state_ret[0], bp, tp, n_p // n_s, cols)
    or_p, s_ret_p = _ret_prompt(proj, cos_p, sin_p, bp, tp, dk, dv, cols[4], cols[5], cols[6], cols[7])
    s_gla_p = jnp.swapaxes(st_gla_p, -1, -2)

    tm_p = min(256, n_p)
    h1, lt_p = _out_proj(og_p, or_p, xp, wo, nffn, rwt, rb, tm_p, n, 0)
    h1, lt_s = _out_proj(og_s, or_s, xs, wo, nffn, rwt, rb, n_s, n, n_p // n_s, h1_prev=h1)
    logits_t = jnp.concatenate([lt_p, lt_s], axis=1)

    n_slots = (n * TOP_K) // SLOT_ROWS + ne
    n_tiles_max = (n * TOP_K) // MM_ROWS + ne
    dest, probs_t, item_meta, tile_meta = _route(logits_t, n_tiles_max)
    pad_lo = item_meta[4, :ne]
    pad_hi = item_meta[5, :ne]
    dest_flat = dest.reshape(-1)
    experts = item_meta[0, :n_slots]
    nsub = item_meta[1, :n_slots]
    counts = item_meta[2:4, 0]
    tile_ids = tile_meta[0, :n_tiles_max]

    xs_rows = _gather(dest_flat, tile_ids, counts, pad_lo, pad_hi, h1, nffn, n_slots, n_tiles_max)
    hid = _moe_up(experts, nsub, counts, xs_rows.reshape(n_slots, SLOT_ROWS, d), w_gate[0], w_up[0],
                  b_gate[0][:, None, :], b_up[0][:, None, :], tn=256)
    y_rows = _moe_down(experts, nsub, counts, hid, w_down[0], b_down[0][:, None, :], tn=512)

    y_p, y_s = _combine(dest_flat, y_rows.reshape(n_slots * SLOT_ROWS, d), probs_t.T, h1,
                        norm_final[None, :], n_p, n_s)
    return (y_p.reshape(bp, tp, d), y_s.reshape(bs, ts, d),
            s_gla_p[None], s_ret_p[None], s_gla_s[None], s_ret_s[None])
```

```python
import functools
import math

import numpy as np
import jax
import jax.numpy as jnp
from jax import lax
from jax.experimental import pallas as pl
from jax.experimental.pallas import tpu as pltpu

F32 = jnp.float32
BF16 = jnp.bfloat16
I32 = jnp.int32

HEADS = 4
GLA_RANK = 16
GLA_GATE_NORM = 16.0
ROPE_BASE = 10000.0
TOP_K = 4
SWIGLU_LIMIT = 7.0
SWIGLU_ALPHA = 1.702
EPS = 1e-6
PAST_LEN = 16384

LANE = 128
ROW_TILE = 128
SLOT_TILES = 10
SLOT_ROWS = SLOT_TILES * ROW_TILE
MM_ROWS = 2 * ROW_TILE
SLOT_CHUNKS = SLOT_ROWS // MM_ROWS
GLA_CHUNK = 64
GLA_SUB = 16
RET_CHUNK = 256
VMEM_LIMIT = 60 * 1024 * 1024

_NT = (((1,), (1,)), ((), ()))
_TN = (((0,), (0,)), ((), ()))


def _params(sem, vmem=VMEM_LIMIT):
    return pltpu.CompilerParams(dimension_semantics=sem, vmem_limit_bytes=vmem)


def _sigmoid(x):
    return 1.0 / (1.0 + jnp.exp(-x))


def _rms_scale(x):
    return lax.rsqrt(jnp.mean(x * x, axis=-1, keepdims=True) + EPS)


def _norm_gate_kernel(x_ref, nw_ref, wr_ref, wgk_ref, bgk_ref, *rest):
    h_ref, lg_ref = rest[-2], rest[-1]
    x = x_ref[...]
    hb = (x * _rms_scale(x) * nw_ref[...]).astype(BF16)
    h_ref[...] = hb
    gr = lax.dot_general(hb, wr_ref[...].astype(BF16), _NT, preferred_element_type=F32)
    z = jnp.dot(gr.astype(BF16), wgk_ref[...], preferred_element_type=F32) + bgk_ref[...]
    log_sig = jnp.minimum(z, 0.0) - jnp.log1p(jnp.exp(-jnp.abs(z)))
    lg_ref[...] = log_sig * (1.0 / GLA_GATE_NORM)


def _norm_gate(x, nw, w_t, rank_col, wgk, bgk, tm, n_total, row_block0, prev=None):
    m, d = x.shape
    assert m % tm == 0 and rank_col % LANE == 0
    nlg = wgk.shape[1]
    in_specs = [
        pl.BlockSpec((tm, d), lambda i: (i, 0)),
        pl.BlockSpec((1, d), lambda i: (0, 0)),
        pl.BlockSpec((None, LANE, d), lambda i: (0, rank_col // LANE, 0)),
        pl.BlockSpec((LANE, nlg), lambda i: (0, 0)),
        pl.BlockSpec((1, nlg), lambda i: (0, 0)),
    ]
    args = [x, nw, w_t, wgk, bgk]
    aliases = {}
    if prev is not None:
        in_specs += [pl.BlockSpec(memory_space=pl.ANY)] * 2
        args += list(prev)
        aliases = {len(args) - 2: 0, len(args) - 1: 1}
    return pl.pallas_call(
        _norm_gate_kernel,
        grid=(m // tm,),
        in_specs=in_specs,
        out_specs=[pl.BlockSpec((tm, d), lambda i: (row_block0 + i, 0)),
                   pl.BlockSpec((tm, nlg), lambda i: (row_block0 + i, 0))],
        out_shape=[jax.ShapeDtypeStruct((n_total, d), BF16), jax.ShapeDtypeStruct((n_total, nlg), F32)],
        input_output_aliases=aliases,
        compiler_params=_params(("parallel",)),
        name="norm_gate",
    )(*args)


def _in_proj_kernel(h_ref, wm_ref, wx_ref, proj_ref, w_scr, *, first_shifted):
    j = pl.program_id(0)
    tn = wm_ref.shape[0]

    first = pl.program_id(1) == 0

    def project():
        proj_ref[...] = lax.dot_general(h_ref[...], w_scr[...], _NT, preferred_element_type=F32)

    @pl.when(jnp.logical_and(first, j < first_shifted))
    def _():
        w_scr[...] = wm_ref[...].astype(BF16)
        project()

    @pl.when(jnp.logical_and(first, j >= first_shifted))
    def _():
        w_scr[:tn - GLA_RANK, :] = wm_ref[GLA_RANK:, :].astype(BF16)
        w_scr[tn - GLA_RANK:, :] = wx_ref[...].astype(BF16)
        project()

    @pl.when(jnp.logical_not(first))
    def _():
        project()


def _in_proj(h, w_t, rank_col, tm, tn):
    m, d = h.shape
    n = w_t.shape[1] - GLA_RANK
    assert m % tm == 0 and n % tn == 0 and rank_col % tn == 0 and tn % GLA_RANK == 0
    per = tn // GLA_RANK
    return pl.pallas_call(
        functools.partial(_in_proj_kernel, first_shifted=rank_col // tn),
        grid=(n // tn, m // tm),
        in_specs=[
            pl.BlockSpec((tm, d), lambda j, i: (i, 0)),
            pl.BlockSpec((None, tn, d), lambda j, i: (0, j, 0)),
            pl.BlockSpec((None, GLA_RANK, d), lambda j, i: (0, (j + 1) * per, 0)),
        ],
        out_specs=pl.BlockSpec((tm, tn), lambda j, i: (i, j)),
        out_shape=jax.ShapeDtypeStruct((m, n), F32),
        scratch_shapes=[pltpu.VMEM((tn, d), BF16)],
        compiler_params=_params(("arbitrary", "arbitrary")),
        name="in_proj",
    )(h, w_t, w_t)


def _rope(x, cos, sin_signed, even):
    n = x.shape[-1]
    nxt = pltpu.roll(x, n - 1, axis=x.ndim - 1)
    prv = pltpu.roll(x, 1, axis=x.ndim - 1)
    return x * cos + jnp.where(even, nxt, prv) * sin_signed


def _ret_prompt_kernel(q_ref, k_ref, v_ref, g_ref, cos_ref, sin_ref, o_ref, s_ref):
    c = q_ref.shape[0]
    dk = q_ref.shape[1] // HEADS
    dv = v_ref.shape[1] // HEADS

    @pl.when(pl.program_id(1) == 0)
    def _():
        s_ref[...] = jnp.zeros_like(s_ref)

    ti = lax.broadcasted_iota(I32, (c, c), 0)
    si = lax.broadcasted_iota(I32, (c, c), 1)
    dlt = (ti - si).astype(F32)
    causal = ti >= si
    tcol = lax.broadcasted_iota(I32, (c, 1), 0).astype(F32)
    even = (lax.broadcasted_iota(I32, (c, dk), 1) % 2) == 0
    cos = cos_ref[...]
    sin = sin_ref[...]
    for h in range(HEADS):
        lgam = math.log(1.0 - 2.0 ** (-5.0 - h))
        q = _rope(q_ref[:, h * dk:(h + 1) * dk], cos, sin, even)
        k = _rope(k_ref[:, h * dk:(h + 1) * dk], cos, sin, even) * (dk ** -0.5)
        qb = q.astype(BF16)
        vb = v_ref[:, h * dv:(h + 1) * dv].astype(BF16)
        decay = jnp.where(causal, jnp.exp(dlt * lgam), 0.0)
        a = lax.dot_general(qb, k.astype(BF16), _NT, preferred_element_type=F32) * decay
        s = s_ref[0, h]
        o = jnp.dot(a.astype(BF16), vb, preferred_element_type=F32)
        o = o + jnp.dot(qb, s.astype(BF16), preferred_element_type=F32) * jnp.exp((tcol + 1.0) * lgam)
        kd = (k * jnp.exp((c - 1.0 - tcol) * lgam)).astype(BF16)
        s_ref[0, h] = math.exp(c * lgam) * s + lax.dot_general(kd, vb, _TN, preferred_element_type=F32)
        g = g_ref[:, h * dv:(h + 1) * dv]
        o_ref[:, h * dv:(h + 1) * dv] = (o * _rms_scale(o) * (g * _sigmoid(g))).astype(BF16)


def _ret_prompt(proj, cos, sin, batch, seq, dk, dv, col_q, col_k, col_v, col_g):
    c = math.gcd(seq, RET_CHUNK)
    nt = seq // c
    qw, vw = HEADS * dk, HEADS * dv
    return pl.pallas_call(
        _ret_prompt_kernel,
        grid=(batch, nt),
        in_specs=[
            pl.BlockSpec((c, qw), lambda b, t: (b * nt + t, col_q // qw)),
            pl.BlockSpec((c, qw), lambda b, t: (b * nt + t, col_k // qw)),
            pl.BlockSpec((c, vw), lambda b, t: (b * nt + t, col_v // vw)),
            pl.BlockSpec((c, vw), lambda b, t: (b * nt + t, col_g // vw)),
            pl.BlockSpec((c, dk), lambda b, t: (t, 0)),
            pl.BlockSpec((c, dk), lambda b, t: (t, 0)),
        ],
        out_specs=[
            pl.BlockSpec((c, vw), lambda b, t: (b * nt + t, 0)),
            pl.BlockSpec((1, HEADS, dk, dv), lambda b, t: (b, 0, 0, 0)),
        ],
        out_shape=[jax.ShapeDtypeStruct((batch * seq, vw), BF16),
                   jax.ShapeDtypeStruct((batch, HEADS, dk, dv), F32)],
        compiler_params=_params(("parallel", "arbitrary")),
        name="ret_prompt",
    )(proj, proj, proj, proj, cos, sin)


def _gla_chunk(q_ref, k_ref, v_ref, g_ref, lg_ref, nw_ref, o_ref, st_ref, b_scr, a_scr):
    c = q_ref.shape[0]
    dk = q_ref.shape[1] // HEADS
    dv = v_ref.shape[1] // HEADS
    sub = GLA_SUB
    nsub = c // sub

    @pl.when(pl.program_id(1) == 0)
    def _():
        st_ref[...] = jnp.zeros_like(st_ref)

    ri = lax.broadcasted_iota(I32, (c, c), 0)
    ci = lax.broadcasted_iota(I32, (c, c), 1)
    lower = (ri >= ci).astype(BF16)
    cis = lax.broadcasted_iota(I32, (sub, c), 1)

    for h in range(HEADS):
        lg = lg_ref[:, h * dk:(h + 1) * dk]
        l1 = lg.astype(BF16)
        r1 = lg - l1.astype(F32)
        l2 = r1.astype(BF16)
        l3 = (r1 - l2.astype(F32)).astype(BF16)
        b_scr[h] = (jnp.dot(lower, l1, preferred_element_type=F32)
                    + jnp.dot(lower, l2, preferred_element_type=F32)
                    + jnp.dot(lower, l3, preferred_element_type=F32))

    def sub_block(i, carry):
        r0 = pl.multiple_of(i * sub, sub)
        rprev = jnp.maximum(r0 - 1, 0)
        for h in range(HEADS):
            hs = slice(h * dk, (h + 1) * dk)
            qi = q_ref[pl.ds(r0, sub), hs] * (dk ** -0.5)
            ki = k_ref[pl.ds(r0, sub), hs]
            bi = b_scr[h, pl.ds(r0, sub), :]
            ad = jnp.zeros((sub, c), F32)
            for s in range(sub):
                e = jnp.exp(jnp.minimum(bi - bi[s:s + 1, :], 0.0))
                col = jnp.sum(qi * e * ki[s:s + 1, :], axis=1, keepdims=True)
                ad = jnp.where(cis == r0 + s, col, ad)
            bref = b_scr[h, pl.ds(rprev, 1), :]
            qq = (qi * jnp.exp(jnp.minimum(bi - bref, 0.0))).astype(BF16)
            kk = (k_ref[:, hs] * jnp.exp(jnp.minimum(bref - b_scr[h], 0.0))).astype(BF16)
            ao = lax.dot_general(qq, kk, _NT, preferred_element_type=F32)
            a_scr[h, pl.ds(r0, sub), :] = jnp.where(cis < r0, ao, ad)
        return carry

    lax.fori_loop(0, nsub, sub_block, 0)

    for h in range(HEADS):
        hs = slice(h * dk, (h + 1) * dk)
        vs = slice(h * dv, (h + 1) * dv)
        b = b_scr[h]
        b_last = b[c - 1:c, :]
        vb = v_ref[:, vs].astype(BF16)
        qe = (q_ref[:, hs] * (dk ** -0.5) * jnp.exp(b)).astype(BF16)
        kd = (k_ref[:, hs] * jnp.exp(b_last - b)).astype(BF16)
        st = st_ref[0, h]
        a = jnp.where(ri >= ci, a_scr[h], 0.0)
        o = jnp.dot(a.astype(BF16), vb, preferred_element_type=F32)
        o = o + lax.dot_general(qe, st.astype(BF16), _NT, preferred_element_type=F32)
        st_ref[0, h] = st * jnp.exp(b_last) + lax.dot_general(vb, kd, _TN, preferred_element_type=F32)
        g = g_ref[:, vs]
        o_ref[:, vs] = (o * _rms_scale(o) * nw_ref[...] * (g * _sigmoid(g))).astype(BF16)


def _to_col(row, eye):
    return jnp.sum(jnp.where(eye, row, 0.0), axis=1, keepdims=True)


def _step_rows(b, seq, proj_ref, lg_ref, cos_ref, sin_ref, nw_ref, sg_ref, sr_ref,
               og_ref, or_ref, ng_ref, nr_ref, cols):
    dk, dv = sg_ref.shape[-2], sg_ref.shape[-1]
    col_gq, col_gk, col_gv, col_gg, col_rq, col_rk, col_rv, col_rg = cols
    eye = lax.broadcasted_iota(I32, (dk, dk), 0) == lax.broadcasted_iota(I32, (dk, dk), 1)
    even = (lax.broadcasted_iota(I32, (1, dk), 1) % 2) == 0
    row = pl.ds(b, 1)

    def finish(o, g, w):
        o = o * _rms_scale(o)
        if w is not None:
            o = o * w
        return o * (g * _sigmoid(g))

    for h in range(HEADS):
        q = proj_ref[row, col_gq + h * dk:col_gq + (h + 1) * dk] * (dk ** -0.5)
        k = proj_ref[row, col_gk + h * dk:col_gk + (h + 1) * dk]
        v = proj_ref[row, col_gv + h * dv:col_gv + (h + 1) * dv]
        g = proj_ref[row, col_gg + h * dv:col_gg + (h + 1) * dv]
        a = jnp.exp(lg_ref[row, h * dk:(h + 1) * dk])
        s_new = _to_col(a, eye) * sg_ref[seq, h] + _to_col(k, eye) * v
        ng_ref[seq, h] = s_new
        o = jnp.sum(_to_col(q, eye) * s_new, axis=0, keepdims=True)
        og_ref[row, h * dv:(h + 1) * dv] = finish(o, g, nw_ref[...])

        gamma = 1.0 - 2.0 ** (-5.0 - h)
        q = _rope(proj_ref[row, col_rq + h * dk:col_rq + (h + 1) * dk], cos_ref[...], sin_ref[...], even)
        k = _rope(proj_ref[row, col_rk + h * dk:col_rk + (h + 1) * dk], cos_ref[...], sin_ref[...], even)
        k = k * (dk ** -0.5)
        v = proj_ref[row, col_rv + h * dv:col_rv + (h + 1) * dv]
        g = proj_ref[row, col_rg + h * dv:col_rg + (h + 1) * dv]
        s_new = gamma * sr_ref[seq, h] + _to_col(k, eye) * v
        nr_ref[seq, h] = s_new
        o = jnp.sum(_to_col(q, eye) * s_new, axis=0, keepdims=True)
        or_ref[row, h * dv:(h + 1) * dv] = finish(o, g, None)


def _gla_step_kernel(q_ref, k_ref, v_ref, g_ref, lg_ref, nw_ref, sproj_ref, slg_ref, cos_ref, sin_ref,
                     sg_ref, sr_ref, o_ref, st_ref, og_ref, or_ref, ng_ref, nr_ref, b_scr, a_scr,
                     *, nt, cols):
    _gla_chunk(q_ref, k_ref, v_ref, g_ref, lg_ref, nw_ref, o_ref, st_ref, b_scr, a_scr)
    per_step = sg_ref.shape[0]
    first = (pl.program_id(0) * nt + pl.program_id(1)) * per_step
    for s in range(per_step):
        _step_rows(first + s, s, sproj_ref, slg_ref, cos_ref, sin_ref, nw_ref, sg_ref, sr_ref,
                   og_ref, or_ref, ng_ref, nr_ref, cols)


def _gla_step(proj, lg, nw, cos_s, sin_s, s_gla, s_ret, batch, seq, sample_row_block, cols):
    nb, _, dk, dv = s_gla.shape
    c = math.gcd(seq, GLA_CHUNK)
    nt = seq // c
    assert nb % (batch * nt) == 0
    per_step = nb // (batch * nt)
    qw, vw = HEADS * dk, HEADS * dv
    ncol = proj.shape[1]
    col_q, col_k, col_v, col_g = cols[:4]
    blk = lambda width, col: pl.BlockSpec((c, width), lambda b, t: (b * nt + t, col // width))
    full = lambda shape: pl.BlockSpec(shape, lambda b, t: (0,) * len(shape))
    rows = lambda width: pl.BlockSpec((nb, width), lambda b, t: (sample_row_block, 0))
    state = pl.BlockSpec((per_step, HEADS, dk, dv), lambda b, t: (b * nt + t, 0, 0, 0))
    return pl.pallas_call(
        functools.partial(_gla_step_kernel, nt=nt, cols=cols),
        grid=(batch, nt),
        in_specs=[blk(qw, col_q), blk(qw, col_k), blk(vw, col_v), blk(vw, col_g), blk(qw, 0), full((1, dv)),
                  rows(ncol), rows(qw), full((1, dk)), full((1, dk)), state, state],
        out_specs=[
            pl.BlockSpec((c, vw), lambda b, t: (b * nt + t, 0)),
            pl.BlockSpec((1, HEADS, dv, dk), lambda b, t: (b, 0, 0, 0)),
            full((nb, vw)), full((nb, vw)), state, state,
        ],
        out_shape=[jax.ShapeDtypeStruct((batch * seq, vw), BF16),
                   jax.ShapeDtypeStruct((batch, HEADS, dv, dk), F32),
                   jax.ShapeDtypeStruct((nb, vw), F32), jax.ShapeDtypeStruct((nb, vw), F32),
                   jax.ShapeDtypeStruct(s_gla.shape, F32), jax.ShapeDtypeStruct(s_ret.shape, F32)],
        scratch_shapes=[pltpu.VMEM((HEADS, c, dk), F32), pltpu.VMEM((HEADS, c, c), F32)],
        compiler_params=_params(("arbitrary", "arbitrary")),
        name="gla_step",
    )(proj, proj, proj, proj, lg, nw, proj, lg, cos_s, sin_s, s_gla, s_ret)


def _out_proj_kernel(og_ref, or_ref, x_ref, wg_ref, wr_ref, nw_ref, rwt_ref, rb_ref, *rest):
    h1_ref, lt_ref = rest[-2], rest[-1]
    acc = jnp.dot(og_ref[...].astype(BF16), wg_ref[...], preferred_element_type=F32)
    acc = acc + jnp.dot(or_ref[...].astype(BF16), wr_ref[...], preferred_element_type=F32)
    h1 = x_ref[...] + acc
    h1_ref[...] = h1
    flat = (h1 * _rms_scale(h1) * nw_ref[...]).astype(BF16)
    lt_ref[...] = lax.dot_general(rwt_ref[...], flat, _NT, preferred_element_type=F32) + rb_ref[...]


def _out_proj(og, orr, x, wo, nw, rwt, rb, tm, n_total, row_block0, h1_prev=None):
    m, d = x.shape
    half = og.shape[1]
    ne = rwt.shape[0]
    const = lambda shape: pl.BlockSpec(shape, lambda i: (0,) * len(shape), pipeline_mode=pl.Buffered(1))
    in_specs = [
        pl.BlockSpec((tm, half), lambda i: (i, 0)),
        pl.BlockSpec((tm, half), lambda i: (i, 0)),
        pl.BlockSpec((tm, d), lambda i: (i, 0)),
        const((half, d)), pl.BlockSpec((half, d), lambda i: (1, 0), pipeline_mode=pl.Buffered(1)),
        const((1, d)), const((ne, d)), const((ne, 1)),
    ]
    args = [og, orr, x, wo, wo, nw, rwt, rb]
    aliases = {}
    if h1_prev is not None:
        in_specs.append(pl.BlockSpec(memory_space=pl.ANY))
        args.append(h1_prev)
        aliases = {len(args) - 1: 0}
    return pl.pallas_call(
        _out_proj_kernel,
        grid=(m // tm,),
        in_specs=in_specs,
        out_specs=[
            pl.BlockSpec((tm, d), lambda i: (row_block0 + i, 0)),
            pl.BlockSpec((ne, tm), lambda i: (0, i)),
        ],
        out_shape=[jax.ShapeDtypeStruct((n_total, d), F32), jax.ShapeDtypeStruct((ne, m), F32)],
        input_output_aliases=aliases,
        compiler_params=_params(("parallel",)),
        name="out_proj",
    )(*args)


def _route_kernel(lt_ref, dest_ref, prob_ref, item_ref, tile_ref, m_scr, pos_scr):
    ne, n = lt_ref.shape
    nblk = n // LANE
    logits = lt_ref[...]
    eio = lax.broadcasted_iota(I32, (ne, n), 0).astype(F32)
    vals, idxs = [], []
    for _ in range(TOP_K):
        m = jnp.max(logits, axis=0, keepdims=True)
        ik = jnp.min(jnp.where(logits == m, eio, float(ne)), axis=0, keepdims=True)
        vals.append(m)
        idxs.append(ik)
        logits = jnp.where(eio == ik, -jnp.inf, logits)
    ex = [jnp.exp(v - vals[0]) for v in vals]
    den = ex[0] + ex[1] + ex[2] + ex[3]
    for k in range(TOP_K):
        prob_ref[k:k + 1, :] = ex[k] / den
    sel = jnp.zeros((ne, n), F32)
    for ik in idxs:
        sel = jnp.where(eio == ik, 1.0, sel)
    m_scr[...] = sel

    upper = (lax.broadcasted_iota(I32, (LANE, LANE), 0) <= lax.broadcasted_iota(I32, (LANE, LANE), 1)).astype(BF16)

    def prefix(cb, carry):
        c0 = pl.multiple_of(cb * LANE, LANE)
        mc = m_scr[:, pl.ds(c0, LANE)]
        incl = jnp.dot(mc.astype(BF16), upper, preferred_element_type=F32)
        pos_scr[:, pl.ds(c0, LANE)] = incl - mc + carry
        return carry + jnp.sum(mc, axis=1, keepdims=True)

    cnt = lax.fori_loop(0, nblk, prefix, jnp.zeros((ne, 1), F32))

    def ceil_div(x, step, most):
        r = jnp.zeros_like(x)
        for mlt in range(most):
            r = r + jnp.where(x > float(mlt * step), 1.0, 0.0)
        return r

    cnt_l = jnp.broadcast_to(cnt, (ne, LANE))
    n_slot = ceil_div(cnt_l, SLOT_ROWS, -(-n // SLOT_ROWS))
    n_tile = ceil_div(cnt_l, MM_ROWS, -(-n // MM_ROWS))
    strict = (lax.broadcasted_iota(I32, (ne, ne), 0) > lax.broadcasted_iota(I32, (ne, ne), 1)).astype(BF16)
    slot0 = jnp.dot(strict, n_slot.astype(BF16), preferred_element_type=F32)
    tile0 = jnp.dot(strict, n_tile.astype(BF16), preferred_element_type=F32)

    destf = slot0[:, :1] * float(SLOT_ROWS) + pos_scr[...]
    for k in range(TOP_K):
        dk_ = jnp.sum(jnp.where(eio == idxs[k], destf, 0.0), axis=0, keepdims=True)
        dest_ref[k:k + 1, :] = dk_.astype(I32)

    def lookup(table, onehot):
        return jnp.sum(jnp.where(onehot, table, 0.0), axis=0, keepdims=True)

    e_col = lax.broadcasted_iota(I32, (ne, LANE), 0).astype(F32)
    it = lax.broadcasted_iota(I32, (ne, LANE), 1).astype(F32)
    n_items = jnp.sum(n_slot, axis=0, keepdims=True)
    e_it = jnp.minimum(jnp.sum(jnp.where(slot0 + n_slot <= it, 1.0, 0.0), axis=0, keepdims=True), ne - 1.0)
    oh = e_col == e_it
    rows = lookup(cnt_l, oh) - (it[:1] - lookup(slot0, oh)) * float(SLOT_ROWS)
    rows = jnp.clip(rows, 0.0, float(SLOT_ROWS))
    rows = jnp.where(it[:1] < n_items, rows, 0.0)
    item_ref[...] = jnp.zeros_like(item_ref)
    item_ref[0:1, :] = e_it.astype(I32)
    item_ref[1:2, :] = ceil_div(rows, ROW_TILE, SLOT_TILES).astype(I32)
    item_ref[2:3, :] = n_items.astype(I32)
    item_ref[3:4, :] = jnp.sum(n_tile, axis=0, keepdims=True).astype(I32)
    diag = e_col == it
    item_ref[4:5, :] = lookup(slot0 * float(SLOT_ROWS) + cnt_l, diag).astype(I32)
    item_ref[5:6, :] = lookup(slot0 * float(SLOT_ROWS) + n_tile * float(MM_ROWS), diag).astype(I32)

    ntl = tile_ref.shape[1]
    tile_ref[...] = jnp.zeros_like(tile_ref)
    for cb in range(ntl // LANE):
        gi = it + float(cb * LANE)
        e_g = jnp.minimum(jnp.sum(jnp.where(tile0 + n_tile <= gi, 1.0, 0.0), axis=0, keepdims=True), ne - 1.0)
        ohg = e_col == e_g
        tid = lookup(slot0, ohg) * float(SLOT_CHUNKS) + (gi[:1] - lookup(tile0, ohg))
        tile_ref[0:1, cb * LANE:(cb + 1) * LANE] = tid.astype(I32)


def _route(logits_t, n_tiles_max):
    ne, n = logits_t.shape
    ntl = -(-n_tiles_max // LANE) * LANE
    return pl.pallas_call(
        _route_kernel,
        out_shape=[jax.ShapeDtypeStruct((TOP_K, n), I32), jax.ShapeDtypeStruct((TOP_K, n), F32),
                   jax.ShapeDtypeStruct((8, LANE), I32), jax.ShapeDtypeStruct((8, ntl), I32)],
        scratch_shapes=[pltpu.VMEM((ne, n), F32), pltpu.VMEM((ne, n), F32)],
        compiler_params=pltpu.CompilerParams(vmem_limit_bytes=VMEM_LIMIT),
        name="route",
    )(logits_t)


def _gather_kernel(dest_sm, tile_sm, cnt_sm, pad_lo_sm, pad_hi_sm, h1_hbm, nw_ref, xs_ref, inv_sm, buf, sem):
    g = pl.program_id(0)
    n_tiles = cnt_sm[1]
    n_tok = dest_sm.shape[0] // TOP_K
    group = 16

    def issue(gi, slot):
        base = tile_sm[gi] * MM_ROWS

        def body(r, carry):
            src = inv_sm[base + r]
            pltpu.make_async_copy(h1_hbm.at[pl.ds(src, 1)], buf.at[slot, pl.ds(r, 1)], sem.at[slot]).start()
            return carry

        lax.fori_loop(0, MM_ROWS, body, 0, unroll=8)

    @pl.when(g == 0)
    def _():
        spread = (1 << (n_tok.bit_length() - 1)) - 1

        def pad_expert(e, carry):
            def clear(i, c2):
                inv_sm[i] = i & spread
                return c2

            lax.fori_loop(pad_lo_sm[e], pad_hi_sm[e], clear, 0)
            return carry

        lax.fori_loop(0, pad_lo_sm.shape[0], pad_expert, 0)
        for k in range(TOP_K):
            def scatter(t, carry):
                inv_sm[dest_sm[k * n_tok + t]] = t
                return carry

            lax.fori_loop(0, n_tok, scatter, 0, unroll=8)
        issue(0, 0)

    @pl.when(g + 1 < n_tiles)
    def _():
        issue(g + 1, (g + 1) % 2)

    @pl.when(g < n_tiles)
    def _():
        slot = g % 2
        pltpu.make_async_copy(h1_hbm.at[pl.ds(0, MM_ROWS)], buf.at[slot], sem.at[slot]).wait()

        def norm_rows(r, carry):
            r0 = pl.multiple_of(r * group, group)
            x = buf[slot, pl.ds(r0, group), :]
            xs_ref[pl.ds(r0, group), :] = (x * _rms_scale(x) * nw_ref[...]).astype(BF16)
            return carry

        lax.fori_loop(0, MM_ROWS // group, norm_rows, 0, unroll=4)


def _gather(dest_flat, tile_ids, counts, pad_lo, pad_hi, h1, nw, n_slots, n_tiles_max):
    d = h1.shape[1]

    def out_map(g, dest_sm, tile_sm, cnt_sm, *_):
        return (tile_sm[jnp.minimum(g, cnt_sm[1] - 1)], 0)

    grid_spec = pltpu.PrefetchScalarGridSpec(
        num_scalar_prefetch=5,
        grid=(n_tiles_max,),
        in_specs=[pl.BlockSpec(memory_space=pl.ANY),
                  pl.BlockSpec((1, d), lambda g, *_: (0, 0))],
        out_specs=pl.BlockSpec((MM_ROWS, d), out_map),
        scratch_shapes=[pltpu.SMEM((n_slots * SLOT_ROWS,), I32),
                        pltpu.VMEM((2, MM_ROWS, d), F32),
                        pltpu.SemaphoreType.DMA((2,))],
    )
    return pl.pallas_call(
        _gather_kernel,
        grid_spec=grid_spec,
        out_shape=jax.ShapeDtypeStruct((n_slots * SLOT_ROWS, d), BF16),
        compiler_params=_params(("arbitrary",)),
        name="moe_gather",
    )(dest_flat, tile_ids, counts, pad_lo, pad_hi, h1, nw)


def _for_row_chunks(ns, matmul, finish, clear, acc_scr):
    acc_a, acc_b = acc_scr
    odd = jnp.logical_and(ns > 1, (ns & 1) == 1)
    n_full = jnp.maximum(lax.shift_right_logical(ns, 1), 1)
    n_pair = lax.shift_right_logical(n_full - 1, 1)
    row0 = lambda c: pl.multiple_of(c * MM_ROWS, MM_ROWS)
    acc_a[...] = matmul(0, MM_ROWS)

    def pair(p, carry):
        c = 2 * p
        acc_b[...] = matmul(row0(c + 1), MM_ROWS)
        finish(row0(c), MM_ROWS, acc_a[...])
        acc_a[...] = matmul(row0(c + 2), MM_ROWS)
        finish(row0(c + 1), MM_ROWS, acc_b[...])
        return carry

    lax.fori_loop(0, n_pair, pair, 0)
    done = 2 * n_pair
    extra = (n_full - 1) > done
    t0 = pl.multiple_of((ns - 1) * ROW_TILE, ROW_TILE)

    @pl.when(extra)
    def _():
        acc_b[...] = matmul(row0(done + 1), MM_ROWS)
        finish(row0(done), MM_ROWS, acc_a[...])

    def drain(acc, c):
        @pl.when(odd)
        def _():
            tail = matmul(t0, ROW_TILE)
            finish(row0(c), MM_ROWS, acc[...])
            finish(t0, ROW_TILE, tail)

        @pl.when(jnp.logical_not(odd))
        def _():
            finish(row0(c), MM_ROWS, acc[...])

    @pl.when(extra)
    def _():
        drain(acc_b, done + 1)

    @pl.when(jnp.logical_not(extra))
    def _():
        drain(acc_a, done)

    def clear_tile(s, carry):
        clear(pl.multiple_of(s * ROW_TILE, ROW_TILE))
        return carry

    lax.fori_loop(jnp.maximum(ns, MM_ROWS // ROW_TILE), SLOT_TILES, clear_tile, 0)


def _moe_up_kernel(exp_sm, nsub_sm, cnt_sm, x_ref, wg_ref, wu_ref, bg_ref, bu_ref, hid_ref, w_scr, *acc_scr):
    it = pl.program_id(0)
    tn = wg_ref.shape[2]

    @pl.when(it < cnt_sm[0])
    def _():
        w_scr[:, :tn] = wg_ref[0].astype(BF16)
        w_scr[:, tn:] = wu_ref[0].astype(BF16)

        def matmul(r0, rows):
            return jnp.dot(x_ref[0, pl.ds(r0, rows), :], w_scr[...], preferred_element_type=F32)

        def finish(r0, rows, gu):
            g = jnp.minimum(gu[:, :tn] + bg_ref[0], SWIGLU_LIMIT)
            u = jnp.clip(gu[:, tn:] + bu_ref[0], -SWIGLU_LIMIT, SWIGLU_LIMIT)
            hid = (u + 1.0) * g * _sigmoid(SWIGLU_ALPHA * g)
            hid_ref[0, pl.ds(r0, rows), :] = hid.astype(BF16)

        def clear(r0):
            hid_ref[0, pl.ds(r0, ROW_TILE), :] = jnp.zeros((ROW_TILE, tn), BF16)

        _for_row_chunks(nsub_sm[it], matmul, finish, clear, acc_scr)


def _moe_down_kernel(exp_sm, nsub_sm, cnt_sm, h_ref, w_ref, b_ref, y_ref, w_scr, *acc_scr):
    it = pl.program_id(0)
    tn = w_ref.shape[2]

    @pl.when(it < cnt_sm[0])
    def _():
        w_scr[...] = w_ref[0].astype(BF16)

        def matmul(r0, rows):
            return jnp.dot(h_ref[0, pl.ds(r0, rows), :], w_scr[...], preferred_element_type=F32)

        def finish(r0, rows, y):
            y_ref[0, pl.ds(r0, rows), :] = y + b_ref[0]

        def clear(r0):
            y_ref[0, pl.ds(r0, ROW_TILE), :] = jnp.zeros((ROW_TILE, tn), F32)

        _for_row_chunks(nsub_sm[it], matmul, finish, clear, acc_scr)


def _item_maps(nj):
    def eff(it, j, cnt_sm):
        last = cnt_sm[0] - 1
        return jnp.minimum(it, last), jnp.where(it <= last, j, nj - 1)

    def x_map(it, j, exp_sm, nsub_sm, cnt_sm):
        ie, _ = eff(it, j, cnt_sm)
        return (ie, 0, 0)

    def w_map(it, j, exp_sm, nsub_sm, cnt_sm):
        ie, je = eff(it, j, cnt_sm)
        return (exp_sm[ie], 0, je)

    def o_map(it, j, exp_sm, nsub_sm, cnt_sm):
        ie, je = eff(it, j, cnt_sm)
        return (ie, 0, je)

    return x_map, w_map, o_map


def _moe_up(experts, nsub, counts, xs, w_gate, w_up, b_gate, b_up, tn):
    n_slots, rows, d = xs.shape
    dff = w_gate.shape[2]
    nj = dff // tn
    x_map, w_map, o_map = _item_maps(nj)
    grid_spec = pltpu.PrefetchScalarGridSpec(
        num_scalar_prefetch=3,
        grid=(n_slots, nj),
        in_specs=[pl.BlockSpec((1, rows, d), x_map),
                  pl.BlockSpec((1, d, tn), w_map), pl.BlockSpec((1, d, tn), w_map),
                  pl.BlockSpec((1, 1, tn), w_map), pl.BlockSpec((1, 1, tn), w_map)],
        out_specs=pl.BlockSpec((1, rows, tn), o_map),
        scratch_shapes=[pltpu.VMEM((d, 2 * tn), BF16),
                        pltpu.VMEM((MM_ROWS, 2 * tn), F32), pltpu.VMEM((MM_ROWS, 2 * tn), F32)],
    )
    return pl.pallas_call(
        _moe_up_kernel,
        grid_spec=grid_spec,
        out_shape=jax.ShapeDtypeStruct((n_slots, rows, dff), BF16),
        compiler_params=_params(("arbitrary", "arbitrary")),
        name="moe_up",
    )(experts, nsub, counts, xs, w_gate, w_up, b_gate, b_up)


def _moe_down(experts, nsub, counts, hid, w_down, b_down, tn):
    n_slots, rows, dff = hid.shape
    d = w_down.shape[2]
    nj = d // tn
    x_map, w_map, o_map = _item_maps(nj)
    grid_spec = pltpu.PrefetchScalarGridSpec(
        num_scalar_prefetch=3,
        grid=(n_slots, nj),
        in_specs=[pl.BlockSpec((1, rows, dff), x_map),
                  pl.BlockSpec((1, dff, tn), w_map),
                  pl.BlockSpec((1, 1, tn), w_map)],
        out_specs=pl.BlockSpec((1, rows, tn), o_map),
        scratch_shapes=[pltpu.VMEM((dff, tn), BF16),
                        pltpu.VMEM((MM_ROWS, tn), F32), pltpu.VMEM((MM_ROWS, tn), F32)],
    )
    return pl.pallas_call(
        _moe_down_kernel,
        grid_spec=grid_spec,
        out_shape=jax.ShapeDtypeStruct((n_slots, rows, d), F32),
        compiler_params=_params(("arbitrary", "arbitrary")),
        name="moe_down",
    )(experts, nsub, counts, hid, w_down, b_down)


def _combine_kernel(dest_sm, y_hbm, p_ref, h1_ref, nw_ref, yp_ref, ys_ref, buf, sem, *, n_prompt_tiles):
    i = pl.program_id(0)
    n_steps = pl.num_programs(0)
    n_tok = dest_sm.shape[0] // TOP_K

    def issue(ti, slot):
        for k in range(TOP_K):
            def body(r, carry):
                src = dest_sm[k * n_tok + ti * ROW_TILE + r]
                pltpu.make_async_copy(y_hbm.at[pl.ds(src, 1)], buf.at[slot, k, pl.ds(r, 1)], sem.at[slot]).start()
                return carry

            lax.fori_loop(0, ROW_TILE, body, 0, unroll=8)

    @pl.when(i == 0)
    def _():
        issue(0, 0)

    @pl.when(i + 1 < n_steps)
    def _():
        issue(i + 1, (i + 1) % 2)

    slot = i % 2
    for k in range(TOP_K):
        pltpu.make_async_copy(y_hbm.at[pl.ds(0, ROW_TILE)], buf.at[slot, k], sem.at[slot]).wait()
    group = 16

    def combine_into(out_ref):
        def rows(r, carry):
            r0 = pl.multiple_of(r * group, group)
            acc = h1_ref[pl.ds(r0, group), :]
            for k in range(TOP_K):
                acc = acc + p_ref[pl.ds(r0, group), k:k + 1] * buf[slot, k, pl.ds(r0, group), :]
            out_ref[pl.ds(r0, group), :] = acc * _rms_scale(acc) * nw_ref[...]
            return carry

        lax.fori_loop(0, ROW_TILE // group, rows, 0, unroll=4)

    @pl.when(i < n_prompt_tiles)
    def _():
        combine_into(yp_ref)

    @pl.when(i >= n_prompt_tiles)
    def _():
        combine_into(ys_ref)


def _combine(dest_flat, y_rows, probs, h1, nw, n_prompt, n_sample):
    n, d = h1.shape
    npt = n_prompt // ROW_TILE
    grid_spec = pltpu.PrefetchScalarGridSpec(
        num_scalar_prefetch=1,
        grid=(n // ROW_TILE,),
        in_specs=[pl.BlockSpec(memory_space=pl.ANY),
                  pl.BlockSpec((ROW_TILE, TOP_K), lambda i, *_: (i, 0)),
                  pl.BlockSpec((ROW_TILE, d), lambda i, *_: (i, 0)),
                  pl.BlockSpec((1, d), lambda i, *_: (0, 0))],
        out_specs=[pl.BlockSpec((ROW_TILE, d), lambda i, *_: (jnp.minimum(i, npt - 1), 0)),
                   pl.BlockSpec((ROW_TILE, d), lambda i, *_: (jnp.maximum(i - npt, 0), 0))],
        scratch_shapes=[pltpu.VMEM((2, TOP_K, ROW_TILE, d), F32), pltpu.SemaphoreType.DMA((2,))],
    )
    return pl.pallas_call(
        functools.partial(_combine_kernel, n_prompt_tiles=npt),
        grid_spec=grid_spec,
        out_shape=[jax.ShapeDtypeStruct((n_prompt, d), F32), jax.ShapeDtypeStruct((n_sample, d), F32)],
        compiler_params=_params(("arbitrary",)),
        name="moe_combine",
    )(dest_flat, y_rows, probs, h1, nw)


def _row_tile(n, cap, mult):
    return max(t for t in range(mult, min(n, cap) + 1, mult) if n % t == 0)


def _rope_tables(pos0, t, dk):
    inv = 1.0 / (ROPE_BASE ** jnp.linspace(0.0, 1.0, dk // 2, dtype=F32))
    ang = (pos0 + jnp.arange(t, dtype=F32))[:, None] * inv[None, :]
    cos = jnp.repeat(jnp.cos(ang), 2, axis=-1)
    sin = jnp.repeat(jnp.sin(ang), 2, axis=-1)
    sign = jnp.where(jnp.arange(dk) % 2 == 0, -1.0, 1.0).astype(F32)
    return cos, sin * sign


def kernel(x_prompt, x_sample, state_gla, state_ret, norm_mix, w_in, gla_w_gk, gla_b_gk, gla_norm_w, w_out,
           norm_ffn, router_w, router_b, w_gate, b_gate, w_up, b_up, w_down, b_down, norm_final):
    bp, tp, d = x_prompt.shape
    bs, ts, _ = x_sample.shape
    assert ts == 1 and w_in.shape[0] == 1
    dk, dv = state_gla.shape[-2], state_gla.shape[-1]
    qk, vv = HEADS * dk, HEADS * dv
    n_p, n_s = bp * tp, bs * ts
    n = n_p + n_s
    ne = router_w.shape[-1]
    assert n_p % ROW_TILE == 0 and n_s % ROW_TILE == 0

    c_rank = 2 * qk + 2 * vv
    w_gk = jnp.pad(gla_w_gk[0], ((0, LANE - GLA_RANK), (0, 0))).astype(BF16)
    b_gk = gla_b_gk[0][None, :]
    cols = (0, qk, 2 * qk, 2 * qk + vv, c_rank, c_rank + qk, c_rank + 2 * qk, c_rank + 2 * qk + vv)
    nmix = norm_mix[0][None, :]
    wo = w_out[0].astype(BF16)
    rwt = router_w[0].T.astype(BF16)
    rb = router_b[0][:, None]
    nffn = norm_ffn[0][None, :]
    gnw = gla_norm_w[0][None, :]

    xp = x_prompt.reshape(n_p, d)
    xs = x_sample.reshape(n_s, d)
    w_t = jnp.swapaxes(w_in, 1, 2)
    h_lg = _norm_gate(xp, nmix, w_t, c_rank, w_gk, b_gk, _row_tile(n_p, 512, 16), n, 0)
    h, lg = _norm_gate(xs, nmix, w_t, c_rank, w_gk, b_gk, n_s, n, n_p // n_s, prev=h_lg)
    proj = _in_proj(h, w_t, c_rank, tm=_row_tile(n, 1088, 16), tn=512)

    cos_p, sin_p = _rope_tables(0.0, tp, dk)
    cos_s, sin_s = _rope_tables(float(PAST_LEN), 1, dk)
    og_p, st_gla_p, og_s, or_s, s_gla_s, s_ret_s = _gla_step(
        proj, lg, gnw, cos_s, sin_s, state_gla[0], state_ret[0], bp, tp, n_p // n_s, cols)
    or_p, s_ret_p = _ret_prompt(proj, cos_p, sin_p, bp, tp, dk, dv, cols[4], cols[5], cols[6], cols[7])
    s_gla_p = jnp.swapaxes(st_gla_p, -1, -2)

    tm_p = min(256, n_p)
    h1, lt_p = _out_proj(og_p, or_p, xp, wo, nffn, rwt, rb, tm_p, n, 0)
    h1, lt_s = _out_proj(og_s, or_s, xs, wo, nffn, rwt, rb, n_s, n, n_p // n_s, h1_prev=h1)
    logits_t = jnp.concatenate([lt_p, lt_s], axis=1)

    n_slots = (n * TOP_K) // SLOT_ROWS + ne
    n_tiles_max = (n * TOP_K) // MM_ROWS + ne
    dest, probs_t, item_meta, tile_meta = _route(logits_t, n_tiles_max)
    pad_lo = item_meta[4, :ne]
    pad_hi = item_meta[5, :ne]
    dest_flat = dest.reshape(-1)
    experts = item_meta[0, :n_slots]
    nsub = item_meta[1, :n_slots]
    counts = item_meta[2:4, 0]
    tile_ids = tile_meta[0, :n_tiles_max]

    xs_rows = _gather(dest_flat, tile_ids, counts, pad_lo, pad_hi, h1, nffn, n_slots, n_tiles_max)
    hid = _moe_up(experts, nsub, counts, xs_rows.reshape(n_slots, SLOT_ROWS, d), w_gate[0], w_up[0],
                  b_gate[0][:, None, :], b_up[0][:, None, :], tn=256)
    y_rows = _moe_down(experts, nsub, counts, hid, w_down[0], b_down[0][:, None, :], tn=512)

    y_p, y_s = _combine(dest_flat, y_rows.reshape(n_slots * SLOT_ROWS, d), probs_t.T, h1,
                        norm_final[None, :], n_p, n_s)
    return (y_p.reshape(bp, tp, d), y_s.reshape(bs, ts, d),
            s_gla_p[None], s_ret_p[None], s_gla_s[None], s_ret_s[None])
```
